```python
import math
import jax
import jax.numpy as jnp
from jax import lax
import numpy as np

D_MODEL = 2048
BATCH = 2
SEQ = 16384
DEPTH = 1

N_MEM = 256
MIX_WIDTH = D_MODEL
GLA_WIDTH = MIX_WIDTH // 2
DIFF_WIDTH = MIX_WIDTH // 4
MEM_WIDTH = MIX_WIDTH // 4

GLA_HEADS = 4
GLA_DV = GLA_WIDTH // GLA_HEADS
GLA_DK = GLA_DV // 2
GLA_LOWRANK = 16
GLA_TAU = 16.0
GLA_CHUNK = 64
GLA_NORM_EPS = 1e-6

DIFF_HEADS = 4
DIFF_DV = DIFF_WIDTH // DIFF_HEADS
DIFF_DQK = DIFF_DV // 2
DIFF_NORM_EPS = 1e-5
Q_BLOCK = 128

MEM_HEADS = 4
MEM_DH = MEM_WIDTH // MEM_HEADS

ROPE_THETA = 500000.0
ROT_DIM = DIFF_DQK // 4
LN_EPS = 1e-5
DEEPNORM_ALPHA = (2.0 * DEPTH) ** 0.25
DEEPNORM_BETA = (8.0 * DEPTH) ** -0.25

IN_SIZES = (
    GLA_HEADS * GLA_DK,
    GLA_HEADS * GLA_DK,
    GLA_WIDTH,
    GLA_WIDTH,
    GLA_LOWRANK,
    DIFF_WIDTH,
    DIFF_WIDTH,
    DIFF_WIDTH,
    DIFF_WIDTH,
    MEM_WIDTH,
    MEM_WIDTH,
)
IN_WIDTH = int(sum(IN_SIZES))
IN_SPLITS = tuple(int(s) for s in np.cumsum(IN_SIZES)[:-1])

kernel_name = 'hybrid_gla_diffattn_memxattn_deepnorm'


def _lambda_init(layer):
    return 0.8 - 0.6 * math.exp(-0.3 * layer)


def _layernorm(x, g, b):
    xf = x.astype(jnp.float32)
    mu = jnp.mean(xf, axis=-1, keepdims=True)
    var = jnp.mean(jnp.square(xf - mu), axis=-1, keepdims=True)
    y = (xf - mu) * lax.rsqrt(var + LN_EPS) * g.astype(jnp.float32) + b.astype(jnp.float32)
    return y.astype(x.dtype)


def _rmsnorm(x, g, eps):
    xf = x.astype(jnp.float32)
    y = xf * lax.rsqrt(jnp.mean(xf * xf, axis=-1, keepdims=True) + eps) * g.astype(jnp.float32)
    return y.astype(x.dtype)


def _rotary_tables(positions):
    half = jnp.arange(0, ROT_DIM, 2, dtype=jnp.float32) / ROT_DIM
    inv_freq = jnp.power(jnp.float32(ROPE_THETA), -half)
    ang = positions.astype(jnp.float32)[..., None] * inv_freq
    return jnp.cos(ang)[:, :, None, :], jnp.sin(ang)[:, :, None, :]


def _partial_rotary(x, cos, sin):
    half = ROT_DIM // 2
    x1 = x[..., :half].astype(jnp.float32)
    x2 = x[..., half:ROT_DIM].astype(jnp.float32)
    r1 = (x1 * cos - x2 * sin).astype(x.dtype)
    r2 = (x2 * cos + x1 * sin).astype(x.dtype)
    return jnp.concatenate([r1, r2, x[..., ROT_DIM:]], axis=-1)


def _gla_chunked(q, k, v, log_a):
    B, S, H, Dk = q.shape
    Dv = v.shape[-1]
    C = GLA_CHUNK
    nc = S // C

    def chunked(t):
        return t.astype(jnp.float32).reshape(B, nc, C, H, t.shape[-1]).transpose(1, 0, 3, 2, 4)

    qc = chunked(q * (GLA_DK ** -0.5))
    kc = chunked(k)
    vc = chunked(v)
    bc = lax.cumsum(chunked(log_a), axis=3)
    causal = jnp.tril(jnp.ones((C, C), dtype=bool))

    def step(state, inp):
        qi, ki, vi, bi = inp
        o_inter = jnp.einsum('bhck,bhkv->bhcv', qi * jnp.exp(bi), state)
        rel = bi[:, :, :, None, :] - bi[:, :, None, :, :]
        decay = jnp.exp(jnp.where(causal[:, :, None], rel, -jnp.inf))
        scores = jnp.einsum('bhik,bhjk,bhijk->bhij', qi, ki, decay)
        o_intra = jnp.einsum('bhij,bhjv->bhiv', scores, vi)
        b_last = bi[:, :, -1:, :]
        k_dec = ki * jnp.exp(b_last - bi)
        new_state = state * jnp.exp(b_last[:, :, 0, :])[..., None] + jnp.einsum('bhck,bhcv->bhkv', k_dec, vi)
        return new_state, o_inter + o_intra

    state0 = jnp.zeros((B, H, Dk, Dv), jnp.float32)
    _, o = lax.scan(step, state0, (qc, kc, vc, bc))
    return o.transpose(1, 0, 3, 2, 4).reshape(B, S, H, Dv).astype(v.dtype)


def _diff_attention(q, k, v, lam):
    B, S, H, _, Dqk = q.shape
    Dv = v.shape[-1]
    nb = S // Q_BLOCK
    qb = (q * (Dqk ** -0.5)).reshape(B, nb, Q_BLOCK, H, 2, Dqk).transpose(1, 0, 2, 3, 4, 5)
    key_pos = jnp.arange(S)

    def block(args):
        qi, i = args
        s = jnp.einsum('bqhcd,bkhcd->bhcqk', qi, k, preferred_element_type=jnp.float32)
        q_pos = i * Q_BLOCK + jnp.arange(Q_BLOCK)
        mask = key_pos[None, :] <= q_pos[:, None]
        p = jax.nn.softmax(jnp.where(mask, s, -jnp.inf), axis=-1)
        w = p[:, :, 0] - lam * p[:, :, 1]
        return jnp.einsum('bhqk,bkhd->bqhd', w.astype(v.dtype), v)

    out = lax.map(block, (qb, jnp.arange(nb)))
    return out.transpose(1, 0, 2, 3, 4).reshape(B, S, H, Dv)


def _memory_attention(q, mk, mv):
    s = jnp.einsum('bshd,bmhd->bhsm', q * (MEM_DH ** -0.5), mk, preferred_element_type=jnp.float32)
    p = jax.nn.softmax(s, axis=-1)
    return jnp.einsum('bhsm,bmhd->bshd', p.astype(mv.dtype), mv)


def setup_inputs(seed: int = 0) -> dict:
    key = jax.random.key(seed)
    ks = jax.random.split(key, 16)
    f32 = jnp.float32
    x = jax.random.normal(ks[0], (BATCH, SEQ, D_MODEL), f32)
    mem = jax.random.normal(ks[1], (BATCH, N_MEM, D_MODEL), f32)
    offset = jax.random.randint(ks[2], (BATCH, 1), 0, 4096, dtype=jnp.int32)
    positions = offset + jnp.arange(SEQ, dtype=jnp.int32)[None, :]
    w_in = jax.random.normal(ks[3], (DEPTH, D_MODEL, IN_WIDTH), f32) * D_MODEL ** -0.5
    w_gk_up = jax.random.normal(ks[4], (DEPTH, GLA_LOWRANK, GLA_HEADS * GLA_DK), f32) * GLA_LOWRANK ** -0.5
    b_gk_up = 0.1 * jax.random.normal(ks[5], (DEPTH, GLA_HEADS * GLA_DK), f32)
    gla_norm_g = 1.0 + 0.02 * jax.random.normal(ks[6], (DEPTH, GLA_DV), f32)
    lambda_q1 = 0.1 * jax.random.normal(ks[7], (DEPTH, DIFF_DQK), f32)
    lambda_k1 = 0.1 * jax.random.normal(ks[8], (DEPTH, DIFF_DQK), f32)
    lambda_q2 = 0.1 * jax.random.normal(ks[9], (DEPTH, DIFF_DQK), f32)
    lambda_k2 = 0.1 * jax.random.normal(ks[10], (DEPTH, DIFF_DQK), f32)
    diff_norm_g = 1.0 + 0.02 * jax.random.normal(ks[11], (DEPTH, DIFF_DV), f32)
    w_mem_kv = jax.random.normal(ks[12], (DEPTH, D_MODEL, 2 * MEM_WIDTH), f32) * D_MODEL ** -0.5
    w_out = jax.random.normal(ks[13], (DEPTH, MIX_WIDTH, D_MODEL), f32) * (MIX_WIDTH ** -0.5 * DEEPNORM_BETA)
    ln_g = 1.0 + 0.02 * jax.random.normal(ks[14], (DEPTH, D_MODEL), f32)
    ln_b = 0.02 * jax.random.normal(ks[15], (DEPTH, D_MODEL), f32)
    return {'x': x, 'mem': mem, 'positions': positions, 'w_in': w_in,
            'w_gk_up': w_gk_up, 'b_gk_up': b_gk_up, 'gla_norm_g': gla_norm_g,
            'lambda_q1': lambda_q1, 'lambda_k1': lambda_k1,
            'lambda_q2': lambda_q2, 'lambda_k2': lambda_k2,
            'diff_norm_g': diff_norm_g, 'w_mem_kv': w_mem_kv, 'w_out': w_out,
            'ln_g': ln_g, 'ln_b': ln_b}


def reference(x, mem, positions, w_in, w_gk_up, b_gk_up, gla_norm_g,
              lambda_q1, lambda_k1, lambda_q2, lambda_k2, diff_norm_g,
              w_mem_kv, w_out, ln_g, ln_b):
    B, S, _ = x.shape
    M = mem.shape[1]
    f32 = jnp.float32
    cos, sin = _rotary_tables(positions)
    h = x
    for l in range(DEPTH):
        proj = jnp.einsum('bsd,de->bse', h, w_in[l])
        (g_q, g_k, g_v, g_g, g_lr, d_q, d_k, d_v, d_g, m_q, m_g) = jnp.split(proj, IN_SPLITS, axis=-1)

        gk_logit = (jnp.einsum('bsr,rk->bsk', g_lr, w_gk_up[l]) + b_gk_up[l]).astype(f32)
        log_a = jax.nn.log_sigmoid(gk_logit) / GLA_TAU
        gla = _gla_chunked(g_q.reshape(B, S, GLA_HEADS, GLA_DK),
                           g_k.reshape(B, S, GLA_HEADS, GLA_DK),
                           g_v.reshape(B, S, GLA_HEADS, GLA_DV),
                           log_a.reshape(B, S, GLA_HEADS, GLA_DK))
        gla = _rmsnorm(gla, gla_norm_g[l], GLA_NORM_EPS).reshape(B, S, GLA_WIDTH) * jax.nn.silu(g_g)

        dq = _partial_rotary(d_q.reshape(B, S, 2 * DIFF_HEADS, DIFF_DQK), cos, sin).reshape(B, S, DIFF_HEADS, 2, DIFF_DQK)
        dk = _partial_rotary(d_k.reshape(B, S, 2 * DIFF_HEADS, DIFF_DQK), cos, sin).reshape(B, S, DIFF_HEADS, 2, DIFF_DQK)
        lam_init = _lambda_init(l)
        lam = (jnp.exp(jnp.sum(lambda_q1[l].astype(f32) * lambda_k1[l].astype(f32)))
               - jnp.exp(jnp.sum(lambda_q2[l].astype(f32) * lambda_k2[l].astype(f32))) + lam_init)
        diff = _diff_attention(dq, dk, d_v.reshape(B, S, DIFF_HEADS, DIFF_DV), lam)
        diff = (_rmsnorm(diff, diff_norm_g[l], DIFF_NORM_EPS) * (1.0 - lam_init)).reshape(B, S, DIFF_WIDTH) * jax.nn.silu(d_g)

        mkv = jnp.einsum('bmd,de->bme', mem, w_mem_kv[l])
        m_k, m_v = jnp.split(mkv, 2, axis=-1)
        xat = _memory_attention(m_q.reshape(B, S, MEM_HEADS, MEM_DH),
                                m_k.reshape(B, M, MEM_HEADS, MEM_DH),
                                m_v.reshape(B, M, MEM_HEADS, MEM_DH))
        xat = xat.reshape(B, S, MEM_WIDTH) * jax.nn.silu(m_g)

        mix = jnp.concatenate([gla, diff, xat], axis=-1)
        out = jnp.einsum('bse,ed->bsd', mix, w_out[l])
        h = _layernorm(DEEPNORM_ALPHA * h + out, ln_g[l], ln_b[l])
    return h
```

```python
import functools
import math

import jax
import jax.numpy as jnp
import numpy as np
from jax import lax
from jax.experimental import pallas as pl
from jax.experimental.pallas import tpu as pltpu

F32 = jnp.float32
_ACT = jnp.bfloat16

GLA_HEADS = 4
GLA_DK = 128
GLA_DV = 256
GLA_LOWRANK = 16
GLA_TAU = 16.0
GLA_NORM_EPS = 1e-6
GLA_W = GLA_HEADS * GLA_DV
GLA_KW = GLA_HEADS * GLA_DK

DIFF_HEADS = 4
DIFF_DV = 128
DIFF_DQK = 64
DIFF_NORM_EPS = 1e-5
DIFF_W = DIFF_HEADS * DIFF_DV

MEM_HEADS = 4
MEM_DH = 128
MEM_W = MEM_HEADS * MEM_DH

ROPE_THETA = 500000.0
ROT_DIM = DIFF_DQK // 4
LN_EPS = 1e-5

LANES = 128
SUBLANES = 8

GLA_CHUNK = 64
GLA_LEVELS = (32, 16, 8)
GLA_DIAG = 8

_TM_IN = 256
_R_GLA = 512
_TQ = 512
_TM_OUT = 512
_VMEM_LIMIT = 60 * 1024 * 1024


def _lambda_init(layer):
    return 0.8 - 0.6 * math.exp(-0.3 * layer)


def _silu(g):
    return g / (1.0 + jnp.exp(-g))


def _split_hi_lo(a):
    hi = a.astype(_ACT)
    lo = (a - hi.astype(F32)).astype(_ACT)
    return hi, lo


def _memkv_kernel(mem_ref, w_ref, mkT_ref, mv_ref):
    kv = jnp.dot(mem_ref[...].astype(_ACT), w_ref[...], preferred_element_type=F32)
    mkT_ref[...] = kv[:, :MEM_W].T.astype(_ACT)
    mv_ref[...] = kv[:, MEM_W:].astype(_ACT)


def _memkv(mem2d, w_kv, B, M):
    D = mem2d.shape[1]
    return pl.pallas_call(
        _memkv_kernel,
        grid=(B,),
        in_specs=[pl.BlockSpec((M, D), lambda b: (b, 0)),
                  pl.BlockSpec((D, 2 * MEM_W), lambda b: (0, 0))],
        out_specs=[pl.BlockSpec((None, MEM_W, M), lambda b: (b, 0, 0)),
                   pl.BlockSpec((M, MEM_W), lambda b: (b, 0))],
        out_shape=[jax.ShapeDtypeStruct((B, MEM_W, M), _ACT),
                   jax.ShapeDtypeStruct((B * M, MEM_W), _ACT)],
        compiler_params=pltpu.CompilerParams(vmem_limit_bytes=_VMEM_LIMIT),
        name="memkv",
    )(mem2d, w_kv)


def _inproj_kernel(x_ref, wa_ref, wlr_ref, wd_ref, wm_ref, wgk_ref, bgk_ref, cos_ref, sin_ref,
                   mkT_ref, mv_ref,
                   gq_ref, gk_ref, gv_ref, gg_ref, la_ref, dqT_ref, dk_ref, dvT_ref, dg_ref, xat_ref):
    tm = x_ref.shape[0]
    xb = x_ref[...].astype(_ACT)

    def proj(w_ref, c0, c1):
        return jnp.dot(xb, w_ref[:, c0:c1], preferred_element_type=F32)

    gq_ref[...] = (proj(wa_ref, 0, GLA_KW) * (GLA_DK ** -0.5)).astype(_ACT)
    gk_ref[...] = proj(wa_ref, GLA_KW, 2 * GLA_KW).astype(_ACT)
    gv_ref[...] = proj(wa_ref, 2 * GLA_KW, 2 * GLA_KW + GLA_W).astype(_ACT)
    gg_ref[...] = proj(wa_ref, 2 * GLA_KW + GLA_W, 2 * GLA_KW + 2 * GLA_W).astype(_ACT)

    glr = proj(wlr_ref, 0, LANES)
    hi, lo = _split_hi_lo(glr)
    logit = jnp.dot(jnp.concatenate([hi, hi, lo], axis=1), wgk_ref[...],
                    preferred_element_type=F32) + bgk_ref[...]
    log_sig = jnp.minimum(logit, 0.0) - jnp.log(1.0 + jnp.exp(-jnp.abs(logit)))
    la_ref[...] = log_sig * (1.0 / GLA_TAU)

    lane = lax.broadcasted_iota(jnp.int32, (tm, LANES), 1) & (DIFF_DQK - 1)
    cosv = cos_ref[...]
    sinv = sin_ref[...]
    half = ROT_DIM // 2

    def rope(a):
        up = pltpu.roll(a, LANES - half, 1)
        dn = pltpu.roll(a, half, 1)
        return a * cosv + jnp.where(lane < half, -up, dn) * sinv

    dq = proj(wd_ref, 0, DIFF_W)
    dk = proj(wd_ref, DIFF_W, 2 * DIFF_W)
    for h in range(DIFF_HEADS):
        sl = slice(h * DIFF_DV, (h + 1) * DIFF_DV)
        dqT_ref[sl, :] = (rope(dq[:, sl]) * (DIFF_DQK ** -0.5)).T.astype(_ACT)
        dk_ref[:, sl] = rope(dk[:, sl]).astype(_ACT)
    dvT_ref[...] = proj(wd_ref, 2 * DIFF_W, 3 * DIFF_W).T.astype(_ACT)
    dg_ref[...] = proj(wd_ref, 3 * DIFF_W, 4 * DIFF_W).astype(_ACT)

    mq = proj(wm_ref, 0, MEM_W)
    mg = proj(wm_ref, MEM_W, 2 * MEM_W)
    for h in range(MEM_HEADS):
        sl = slice(h * MEM_DH, (h + 1) * MEM_DH)
        qh = (mq[:, sl] * (MEM_DH ** -0.5)).astype(_ACT)
        s = jnp.dot(qh, mkT_ref[sl, :], preferred_element_type=F32)
        p = jnp.exp(s - jnp.max(s, axis=-1, keepdims=True))
        l = jnp.sum(p, axis=-1, keepdims=True)
        o = jnp.dot(p.astype(_ACT), mv_ref[:, sl], preferred_element_type=F32) / l
        xat_ref[:, sl] = (o * _silu(mg[:, sl])).astype(_ACT)


def _inproj(x2d, wa, wlr, wd, wm, wgk3, bgk, cos_t, sin_t, mkT, mv, B, S, M, tm):
    T, D = x2d.shape
    ns = S // tm
    row = lambda b, i: (b * ns + i, 0)
    const = lambda b, i: (0, 0)
    whole = lambda a: pl.BlockSpec(a.shape, const, pipeline_mode=pl.Buffered(1))
    rows = lambda w: pl.BlockSpec((tm, w), row)
    tr = pl.BlockSpec((None, DIFF_W, tm), lambda b, i: (b, 0, i))
    return pl.pallas_call(
        _inproj_kernel,
        grid=(B, ns),
        in_specs=[rows(D), whole(wa), whole(wlr), whole(wd), whole(wm), whole(wgk3), whole(bgk),
                  rows(LANES), rows(LANES),
                  pl.BlockSpec((None, MEM_W, M), lambda b, i: (b, 0, 0)),
                  pl.BlockSpec((M, MEM_W), lambda b, i: (b, 0))],
        out_specs=[rows(GLA_KW), rows(GLA_KW), rows(GLA_W), rows(GLA_W), rows(GLA_KW),
                   tr, rows(DIFF_W), tr, rows(DIFF_W), rows(MEM_W)],
        out_shape=[jax.ShapeDtypeStruct((T, GLA_KW), _ACT), jax.ShapeDtypeStruct((T, GLA_KW), _ACT),
                   jax.ShapeDtypeStruct((T, GLA_W), _ACT), jax.ShapeDtypeStruct((T, GLA_W), _ACT),
                   jax.ShapeDtypeStruct((T, GLA_KW), F32),
                   jax.ShapeDtypeStruct((B, DIFF_W, S), _ACT), jax.ShapeDtypeStruct((T, DIFF_W), _ACT),
                   jax.ShapeDtypeStruct((B, DIFF_W, S), _ACT), jax.ShapeDtypeStruct((T, DIFF_W), _ACT),
                   jax.ShapeDtypeStruct((T, MEM_W), _ACT)],
        compiler_params=pltpu.CompilerParams(
            dimension_semantics=("arbitrary", "arbitrary"), vmem_limit_bytes=_VMEM_LIMIT),
        name="inproj",
    )(x2d, wa, wlr, wd, wm, wgk3, bgk, cos_t, sin_t, mkT, mv)


def _gla_constants():
    C = GLA_CHUNK
    t = np.arange(C)
    tril = (t[None, :] <= t[:, None]).astype(np.float32)
    blocks = [tril]
    for s in GLA_LEVELS:
        ref = (t // (2 * s)) * (2 * s) + s - 1
        blocks.append(tril - tril[ref])
    dstack = np.concatenate(blocks, axis=0)
    rsel = np.zeros((GLA_DIAG * GLA_DK, C), np.float32)
    for j in range(GLA_DIAG):
        rsel[j * GLA_DK:(j + 1) * GLA_DK, j::GLA_DIAG] = 1.0
    return dstack, rsel


def _gla_kernel(q_ref, k_ref, v_ref, g_ref, la_ref, dstack_ref, rsel_ref, gng_ref, o_ref, st_ref):
    R = q_ref.shape[0]
    C = GLA_CHUNK

    @pl.when(pl.program_id(1) == 0)
    def _():
        st_ref[...] = jnp.zeros_like(st_ref)

    ri = lax.broadcasted_iota(jnp.int32, (C, C), 0)
    ci = lax.broadcasted_iota(jnp.int32, (C, C), 1)
    rk = lax.broadcasted_iota(jnp.int32, (C, GLA_DK), 0)
    level_mask = [(ri // (2 * s)) == (ci // (2 * s)) for s in GLA_LEVELS]
    right_rows = [((rk // s) & 1) == 1 for s in GLA_LEVELS]
    diag_mask = ((ri // GLA_DIAG) == (ci // GLA_DIAG)) & (ci <= ri)
    groups = C // GLA_DIAG
    dstack = dstack_ref[...]
    rsel = rsel_ref[...]
    gng = gng_ref[...]

    def group_bcast(a, j):
        a3 = a.reshape(groups, GLA_DIAG, GLA_DK)
        return jnp.broadcast_to(a3[:, j:j + 1, :], (groups, GLA_DIAG, GLA_DK)).reshape(C, GLA_DK)

    def chunk(c, carry):
        rows = pl.ds(pl.multiple_of(c * C, C), C)
        for h in range(GLA_HEADS):
            sk = slice(h * GLA_DK, (h + 1) * GLA_DK)
            sv = slice(h * GLA_DV, (h + 1) * GLA_DV)
            q = q_ref[rows, sk].astype(F32)
            k = k_ref[rows, sk].astype(F32)
            v = v_ref[rows, sv]
            hi, lo = _split_hi_lo(la_ref[rows, sk])
            e2 = jnp.dot(dstack, jnp.concatenate([hi, lo], axis=1), preferred_element_type=F32)
            e_all = e2[:, :GLA_DK] + e2[:, GLA_DK:]
            b = e_all[0:C]

            a = jnp.zeros((C, C), F32)
            for li in range(len(GLA_LEVELS)):
                x = jnp.exp(-jnp.abs(e_all[(li + 1) * C:(li + 2) * C]))
                ql = jnp.where(right_rows[li], q * x, 0.0).astype(_ACT)
                kl = jnp.where(right_rows[li], 0.0, k * x).astype(_ACT)
                al = lax.dot_general(ql, kl, (((1,), (1,)), ((), ())), preferred_element_type=F32)
                a = jnp.where(level_mask[li], al, a)
            ps = []
            for j in range(GLA_DIAG):
                dec = jnp.exp(jnp.minimum(b - group_bcast(b, j), 0.0))
                ps.append((q * group_bcast(k, j) * dec).astype(_ACT))
            ad = jnp.dot(jnp.concatenate(ps, axis=1), rsel, preferred_element_type=F32)
            a = jnp.where(diag_mask, ad, a)

            st = st_ref[h]
            qe = (q * jnp.exp(b)).astype(_ACT)
            o = lax.dot_general(qe, st.astype(_ACT), (((1,), (1,)), ((), ())), preferred_element_type=F32)
            o = o + jnp.dot(a.astype(_ACT), v, preferred_element_type=F32)
            b_last = b[C - 1:C, :]
            kd = (k * jnp.exp(b_last - b)).astype(_ACT)
            upd = lax.dot_general(v, kd, (((0,), (0,)), ((), ())), preferred_element_type=F32)
            st_ref[h] = st * jnp.exp(b_last) + upd

            ms = jnp.mean(o * o, axis=-1, keepdims=True)
            y = o * lax.rsqrt(ms + GLA_NORM_EPS) * gng
            o_ref[rows, sv] = (y * _silu(g_ref[rows, sv].astype(F32))).astype(_ACT)
        return carry

    lax.fori_loop(0, R // C, chunk, 0)


def _gla(gq, gk, gv, gg, la, gng, B, S, R):
    T = gq.shape[0]
    ns = S // R
    dstack, rsel = _gla_constants()
    dstack = jnp.asarray(dstack, _ACT)
    rsel = jnp.asarray(rsel, _ACT)
    row = lambda b, i: (b * ns + i, 0)
    const = lambda b, i: (0, 0)
    rows = lambda w: pl.BlockSpec((R, w), row)
    whole = lambda a: pl.BlockSpec(a.shape, const)
    return pl.pallas_call(
        _gla_kernel,
        grid=(B, ns),
        in_specs=[rows(GLA_KW), rows(GLA_KW), rows(GLA_W), rows(GLA_W), rows(GLA_KW),
                  whole(dstack), whole(rsel), whole(gng)],
        out_specs=rows(GLA_W),
        out_shape=jax.ShapeDtypeStruct((T, GLA_W), _ACT),
        scratch_shapes=[pltpu.VMEM((GLA_HEADS, GLA_DV, GLA_DK), F32)],
        compiler_params=pltpu.CompilerParams(
            dimension_semantics=("arbitrary", "arbitrary"), vmem_limit_bytes=_VMEM_LIMIT),
        name="gla",
    )(gq, gk, gv, gg, la, dstack, rsel, gng)


def _diff_kernel(qT_ref, k_ref, vT_ref, g_ref, lq1_ref, lk1_ref, lq2_ref, lk2_ref, gcol_ref, o_ref,
                 acc_ref, *, lam_init):
    tq = qT_ref.shape[1]
    qi = pl.program_id(2)
    qT = qT_ref[...]
    comp_row = lax.broadcasted_iota(jnp.int32, qT.shape, 0) < DIFF_DQK
    zero = jnp.zeros_like(qT)
    qcs = (jnp.where(comp_row, qT, zero), jnp.where(comp_row, zero, qT))
    causal = (lax.broadcasted_iota(jnp.int32, (tq, tq), 0)
              <= lax.broadcasted_iota(jnp.int32, (tq, tq), 1))
    acc_ref[...] = jnp.zeros_like(acc_ref)

    def block(ki, stats, masked):
        start = pl.multiple_of(ki * tq, tq)
        kb = k_ref[pl.ds(start, tq), :]
        vb = vT_ref[:, pl.ds(start, tq)]
        new = []
        for c in range(2):
            m_old, l_old = stats[2 * c], stats[2 * c + 1]
            s = jnp.dot(kb, qcs[c], preferred_element_type=F32)
            if masked:
                s = jnp.where(causal, s, -jnp.inf)
            m_new = jnp.maximum(m_old, jnp.max(s, axis=0, keepdims=True))
            alpha = jnp.exp(m_old - m_new)
            p = jnp.exp(s - m_new)
            l_new = alpha * l_old + jnp.sum(p, axis=0, keepdims=True)
            acc_ref[c] = alpha * acc_ref[c] + jnp.dot(vb, p.astype(_ACT), preferred_element_type=F32)
            new += [m_new, l_new]
        return tuple(new)

    neg = jnp.full((1, tq), -jnp.inf, F32)
    zer = jnp.zeros((1, tq), F32)
    stats = lax.fori_loop(0, qi, lambda ki, st: block(ki, st, False), (neg, zer, neg, zer))
    m1, l1, m2, l2 = block(qi, stats, True)

    lam = (jnp.exp(jnp.sum(lq1_ref[...] * lk1_ref[...], axis=-1, keepdims=True))
           - jnp.exp(jnp.sum(lq2_ref[...] * lk2_ref[...], axis=-1, keepdims=True)) + lam_init)
    oT = acc_ref[0] / l1 - lam * (acc_ref[1] / l2)
    ms = jnp.mean(oT * oT, axis=0, keepdims=True)
    yT = oT * lax.rsqrt(ms + DIFF_NORM_EPS) * (gcol_ref[...] * (1.0 - lam_init))
    o_ref[...] = (yT.T * _silu(g_ref[...].astype(F32))).astype(_ACT)


def _diff(dqT, dk, dvT, dg, lq1, lk1, lq2, lk2, gcol, B, S, tq, lam_init):
    T = dk.shape[0]
    nq = S // tq
    small = lambda a: pl.BlockSpec(a.shape, lambda b, h, i: (0, 0))
    blk = pl.BlockSpec((tq, DIFF_DV), lambda b, h, i: (b * nq + i, h))
    return pl.pallas_call(
        functools.partial(_diff_kernel, lam_init=lam_init),
        grid=(B, DIFF_HEADS, nq),
        in_specs=[pl.BlockSpec((None, DIFF_DV, tq), lambda b, h, i: (b, h, i)),
                  pl.BlockSpec((S, DIFF_DV), lambda b, h, i: (b, h)),
                  pl.BlockSpec((None, DIFF_DV, S), lambda b, h, i: (b, h, 0)),
                  blk, small(lq1), small(lk1), small(lq2), small(lk2), small(gcol)],
        out_specs=blk,
        out_shape=jax.ShapeDtypeStruct((T, DIFF_W), _ACT),
        scratch_shapes=[pltpu.VMEM((2, DIFF_DV, tq), F32)],
        compiler_params=pltpu.CompilerParams(
            dimension_semantics=("arbitrary", "arbitrary", "arbitrary"), vmem_limit_bytes=_VMEM_LIMIT),
        name="diffattn",
    )(dqT, dk, dvT, dg, lq1, lk1, lq2, lk2, gcol)


def _outproj_kernel(gla_ref, dif_ref, xat_ref, x_ref, w_ref, g_ref, b_ref, o_ref, *, alpha):
    y = alpha * x_ref[...]
    y = y + jnp.dot(gla_ref[...], w_ref[0:GLA_W, :], preferred_element_type=F32)
    y = y + jnp.dot(dif_ref[...], w_ref[GLA_W:GLA_W + DIFF_W, :], preferred_element_type=F32)
    y = y + jnp.dot(xat_ref[...], w_ref[GLA_W + DIFF_W:GLA_W + DIFF_W + MEM_W, :], preferred_element_type=F32)
    mu = jnp.mean(y, axis=-1, keepdims=True)
    yc = y - mu
    var = jnp.mean(yc * yc, axis=-1, keepdims=True)
    o_ref[...] = yc * lax.rsqrt(var + LN_EPS) * g_ref[...] + b_ref[...]


def _outproj(gla, dif, xat, x2d, w_out, ln_g, ln_b, tm, alpha):
    T, D = x2d.shape
    row = lambda i: (i, 0)
    const = lambda i: (0, 0)
    rows = lambda w: pl.BlockSpec((tm, w), row)
    return pl.pallas_call(
        functools.partial(_outproj_kernel, alpha=alpha),
        grid=(T // tm,),
        in_specs=[rows(GLA_W), rows(DIFF_W), rows(MEM_W), rows(D),
                  pl.BlockSpec(w_out.shape, const, pipeline_mode=pl.Buffered(1)),
                  pl.BlockSpec(ln_g.shape, const), pl.BlockSpec(ln_b.shape, const)],
        out_specs=rows(D),
        out_shape=jax.ShapeDtypeStruct((T, D), F32),
        compiler_params=pltpu.CompilerParams(
            dimension_semantics=("arbitrary",), vmem_limit_bytes=_VMEM_LIMIT),
        name="outproj",
    )(gla, dif, xat, x2d, w_out, ln_g, ln_b)


def _rotary_tables(positions):
    B, S = positions.shape
    half = ROT_DIM // 2
    inv_freq = jnp.power(jnp.float32(ROPE_THETA), -(jnp.arange(0, ROT_DIM, 2, dtype=F32) / ROT_DIM))
    ang = positions.astype(F32)[..., None] * inv_freq
    pad = DIFF_DQK - ROT_DIM
    cos = jnp.concatenate([jnp.cos(ang)] * 2 + [jnp.ones((B, S, pad), F32)], axis=-1)
    sin = jnp.concatenate([jnp.sin(ang)] * 2 + [jnp.zeros((B, S, pad), F32)], axis=-1)
    rep = LANES // DIFF_DQK
    return (jnp.tile(cos, (1, 1, rep)).reshape(B * S, LANES),
            jnp.tile(sin, (1, 1, rep)).reshape(B * S, LANES))


def _tile(default, n):
    t = min(default, n)
    assert n % t == 0, (n, t)
    return t


def kernel(x, mem, positions, w_in, w_gk_up, b_gk_up, gla_norm_g, lambda_q1, lambda_k1, lambda_q2,
           lambda_k2, diff_norm_g, w_mem_kv, w_out, ln_g, ln_b):
    B, S, D = x.shape
    M = mem.shape[1]
    depth = w_in.shape[0]
    T = B * S
    assert S % GLA_CHUNK == 0 and M % LANES == 0
    tm_in, r_gla, tq, tm_out = _tile(_TM_IN, S), _tile(_R_GLA, S), _tile(_TQ, S), _tile(_TM_OUT, S)
    alpha = (2.0 * depth) ** 0.25

    cos_t, sin_t = _rotary_tables(positions)
    mem2d = mem.reshape(B * M, D)
    h = x.reshape(T, D)

    o_lr = 2 * GLA_KW + 2 * GLA_W
    o_d = o_lr + GLA_LOWRANK
    o_m = o_d + 4 * DIFF_W
    for l in range(depth):
        w = w_in[l]
        wa = w[:, :o_lr].astype(_ACT)
        wlr = jnp.pad(w[:, o_lr:o_d], ((0, 0), (0, LANES - GLA_LOWRANK))).astype(_ACT)
        wd = w[:, o_d:o_m].astype(_ACT)
        wm = w[:, o_m:].astype(_ACT)
        wgk = jnp.pad(w_gk_up[l].astype(F32), ((0, LANES - GLA_LOWRANK), (0, 0)))
        wgk_hi, wgk_lo = _split_hi_lo(wgk)
        wgk3 = jnp.concatenate([wgk_hi, wgk_lo, wgk_hi], axis=0)
        bgk = b_gk_up[l].astype(F32).reshape(1, GLA_KW)

        mkT, mv = _memkv(mem2d, w_mem_kv[l].astype(_ACT), B, M)
        gq, gk, gv, gg, la, dqT, dk, dvT, dg, xat = _inproj(
            h, wa, wlr, wd, wm, wgk3, bgk, cos_t, sin_t, mkT, mv, B, S, M, tm_in)
        gla = _gla(gq, gk, gv, gg, la, gla_norm_g[l].astype(F32).reshape(1, GLA_DV), B, S, r_gla)
        row64 = lambda a: a[l].astype(F32).reshape(1, DIFF_DQK)
        dif = _diff(dqT, dk, dvT, dg, row64(lambda_q1), row64(lambda_k1), row64(lambda_q2),
                    row64(lambda_k2), diff_norm_g[l].astype(F32).reshape(DIFF_DV, 1), B, S, tq,
                    _lambda_init(l))
        h = _outproj(gla, dif, xat, h, w_out[l].astype(_ACT), ln_g[l].astype(F32).reshape(1, D),
                     ln_b[l].astype(F32).reshape(1, D), tm_out, alpha)
    return h.reshape(B, S, D)
```

```python
import functools
import math

import jax
import jax.numpy as jnp
import numpy as np
from jax import lax
from jax.experimental import pallas as pl
from jax.experimental.pallas import tpu as pltpu

F32 = jnp.float32
_ACT = jnp.bfloat16

GLA_HEADS = 4
GLA_DK = 128
GLA_DV = 256
GLA_LOWRANK = 16
GLA_TAU = 16.0
GLA_NORM_EPS = 1e-6
GLA_W = GLA_HEADS * GLA_DV
GLA_KW = GLA_HEADS * GLA_DK

DIFF_HEADS = 4
DIFF_DV = 128
DIFF_DQK = 64
DIFF_NORM_EPS = 1e-5
DIFF_W = DIFF_HEADS * DIFF_DV
ACT_SUBLANES = 16
DIFF_VT_ROWS = DIFF_DV + ACT_SUBLANES
LOG2E = math.log2(math.e)

MEM_HEADS = 4
MEM_DH = 128
MEM_W = MEM_HEADS * MEM_DH

ROPE_THETA = 500000.0
ROT_DIM = DIFF_DQK // 4
LN_EPS = 1e-5

LANES = 128
SUBLANES = 8

GLA_CHUNK = 64
GLA_LEVELS = (32, 16, 8)
GLA_DIAG = 8

_TM_IN = 256
_R_GLA = 512
_TQ = 512
_TM_OUT = 512
_VMEM_LIMIT = 60 * 1024 * 1024


def _lambda_init(layer):
    return 0.8 - 0.6 * math.exp(-0.3 * layer)


def _silu(g):
    return g / (1.0 + jnp.exp(-g))


def _split_hi_lo(a):
    hi = a.astype(_ACT)
    lo = (a - hi.astype(F32)).astype(_ACT)
    return hi, lo


def _memkv_kernel(mem_ref, w_ref, mkT_ref, mv_ref):
    kv = jnp.dot(mem_ref[...].astype(_ACT), w_ref[...], preferred_element_type=F32)
    mkT_ref[...] = kv[:, :MEM_W].T.astype(_ACT)
    mv_ref[...] = kv[:, MEM_W:].astype(_ACT)


def _memkv(mem2d, w_kv, B, M):
    D = mem2d.shape[1]
    return pl.pallas_call(
        _memkv_kernel,
        grid=(B,),
        in_specs=[pl.BlockSpec((M, D), lambda b: (b, 0)),
                  pl.BlockSpec((D, 2 * MEM_W), lambda b: (0, 0))],
        out_specs=[pl.BlockSpec((None, MEM_W, M), lambda b: (b, 0, 0)),
                   pl.BlockSpec((M, MEM_W), lambda b: (b, 0))],
        out_shape=[jax.ShapeDtypeStruct((B, MEM_W, M), _ACT),
                   jax.ShapeDtypeStruct((B * M, MEM_W), _ACT)],
        compiler_params=pltpu.CompilerParams(vmem_limit_bytes=_VMEM_LIMIT),
        name="memkv",
    )(mem2d, w_kv)


def _inproj_kernel(x_ref, wa_ref, wlr_ref, wd_ref, wm_ref, wgk_ref, bgk_ref, cos_ref, sin_ref,
                   mkT_ref, mv_ref,
                   gq_ref, gk_ref, gv_ref, gg_ref, la_ref, dqT_ref, dk_ref, dvT_ref, dg_ref, xat_ref):
    tm = x_ref.shape[0]
    xb = x_ref[...].astype(_ACT)

    def proj(w_ref, c0, c1):
        return jnp.dot(xb, w_ref[:, c0:c1], preferred_element_type=F32)

    gq_ref[...] = (proj(wa_ref, 0, GLA_KW) * (GLA_DK ** -0.5)).astype(_ACT)
    gk_ref[...] = proj(wa_ref, GLA_KW, 2 * GLA_KW).astype(_ACT)
    gv_ref[...] = proj(wa_ref, 2 * GLA_KW, 2 * GLA_KW + GLA_W).astype(_ACT)
    gg_ref[...] = proj(wa_ref, 2 * GLA_KW + GLA_W, 2 * GLA_KW + 2 * GLA_W).astype(_ACT)

    glr = proj(wlr_ref, 0, LANES)
    hi, lo = _split_hi_lo(glr)
    logit = jnp.dot(jnp.concatenate([hi, hi, lo], axis=1), wgk_ref[...],
                    preferred_element_type=F32) + bgk_ref[...]
    log_sig = jnp.minimum(logit, 0.0) - jnp.log(1.0 + jnp.exp(-jnp.abs(logit)))
    la_ref[...] = log_sig * (1.0 / GLA_TAU)

    lane = lax.broadcasted_iota(jnp.int32, (tm, LANES), 1) & (DIFF_DQK - 1)
    cosv = cos_ref[...]
    sinv = sin_ref[...]
    half = ROT_DIM // 2

    def rope(a):
        up = pltpu.roll(a, LANES - half, 1)
        dn = pltpu.roll(a, half, 1)
        return a * cosv + jnp.where(lane < half, -up, dn) * sinv

    dq = proj(wd_ref, 0, DIFF_W)
    dk = proj(wd_ref, DIFF_W, 2 * DIFF_W)
    dv = proj(wd_ref, 2 * DIFF_W, 3 * DIFF_W)
    ones = jnp.ones((ACT_SUBLANES, tm), _ACT)
    for h in range(DIFF_HEADS):
        sl = slice(h * DIFF_DV, (h + 1) * DIFF_DV)
        dqT_ref[sl, :] = (rope(dq[:, sl]) * (DIFF_DQK ** -0.5 * LOG2E)).T.astype(_ACT)
        dk_ref[:, sl] = rope(dk[:, sl]).astype(_ACT)
        dvT_ref[h * DIFF_VT_ROWS:h * DIFF_VT_ROWS + DIFF_DV, :] = dv[:, sl].T.astype(_ACT)
        dvT_ref[h * DIFF_VT_ROWS + DIFF_DV:(h + 1) * DIFF_VT_ROWS, :] = ones
    dg_ref[...] = proj(wd_ref, 3 * DIFF_W, 4 * DIFF_W).astype(_ACT)

    mq = proj(wm_ref, 0, MEM_W)
    mg = proj(wm_ref, MEM_W, 2 * MEM_W)
    for h in range(MEM_HEADS):
        sl = slice(h * MEM_DH, (h + 1) * MEM_DH)
        qh = (mq[:, sl] * (MEM_DH ** -0.5)).astype(_ACT)
        s = jnp.dot(qh, mkT_ref[sl, :], preferred_element_type=F32)
        p = jnp.exp(s - jnp.max(s, axis=-1, keepdims=True))
        l = jnp.sum(p, axis=-1, keepdims=True)
        o = jnp.dot(p.astype(_ACT), mv_ref[:, sl], preferred_element_type=F32) / l
        xat_ref[:, sl] = (o * _silu(mg[:, sl])).astype(_ACT)


def _inproj(x2d, wa, wlr, wd, wm, wgk3, bgk, cos_t, sin_t, mkT, mv, B, S, M, tm):
    T, D = x2d.shape
    ns = S // tm
    row = lambda b, i: (b * ns + i, 0)
    const = lambda b, i: (0, 0)
    whole = lambda a: pl.BlockSpec(a.shape, const, pipeline_mode=pl.Buffered(1))
    rows = lambda w: pl.BlockSpec((tm, w), row)
    tr = pl.BlockSpec((None, DIFF_W, tm), lambda b, i: (b, 0, i))
    trv = pl.BlockSpec((None, DIFF_HEADS * DIFF_VT_ROWS, tm), lambda b, i: (b, 0, i))
    return pl.pallas_call(
        _inproj_kernel,
        grid=(B, ns),
        in_specs=[rows(D), whole(wa), whole(wlr), whole(wd), whole(wm), whole(wgk3), whole(bgk),
                  rows(LANES), rows(LANES),
                  pl.BlockSpec((None, MEM_W, M), lambda b, i: (b, 0, 0)),
                  pl.BlockSpec((M, MEM_W), lambda b, i: (b, 0))],
        out_specs=[rows(GLA_KW), rows(GLA_KW), rows(GLA_W), rows(GLA_W), rows(GLA_KW),
                   tr, rows(DIFF_W), trv, rows(DIFF_W), rows(MEM_W)],
        out_shape=[jax.ShapeDtypeStruct((T, GLA_KW), _ACT), jax.ShapeDtypeStruct((T, GLA_KW), _ACT),
                   jax.ShapeDtypeStruct((T, GLA_W), _ACT), jax.ShapeDtypeStruct((T, GLA_W), _ACT),
                   jax.ShapeDtypeStruct((T, GLA_KW), F32),
                   jax.ShapeDtypeStruct((B, DIFF_W, S), _ACT), jax.ShapeDtypeStruct((T, DIFF_W), _ACT),
                   jax.ShapeDtypeStruct((B, DIFF_HEADS * DIFF_VT_ROWS, S), _ACT),
                   jax.ShapeDtypeStruct((T, DIFF_W), _ACT),
                   jax.ShapeDtypeStruct((T, MEM_W), _ACT)],
        compiler_params=pltpu.CompilerParams(
            dimension_semantics=("arbitrary", "arbitrary"), vmem_limit_bytes=_VMEM_LIMIT),
        name="inproj",
    )(x2d, wa, wlr, wd, wm, wgk3, bgk, cos_t, sin_t, mkT, mv)


def _gla_constants():
    C = GLA_CHUNK
    t = np.arange(C)
    tril = (t[None, :] <= t[:, None]).astype(np.float32)
    blocks = [tril]
    for s in GLA_LEVELS:
        ref = (t // (2 * s)) * (2 * s) + s - 1
        blocks.append(tril - tril[ref])
    dstack = np.concatenate(blocks, axis=0)
    rsel = np.zeros((GLA_DIAG * GLA_DK, C), np.float32)
    for j in range(GLA_DIAG):
        rsel[j * GLA_DK:(j + 1) * GLA_DK, j::GLA_DIAG] = 1.0
    return dstack, rsel


def _gla_kernel(q_ref, k_ref, v_ref, g_ref, la_ref, dstack_ref, rsel_ref, gng_ref, o_ref, st_ref):
    R = q_ref.shape[0]
    C = GLA_CHUNK

    @pl.when(pl.program_id(1) == 0)
    def _():
        st_ref[...] = jnp.zeros_like(st_ref)

    ri = lax.broadcasted_iota(jnp.int32, (C, C), 0)
    ci = lax.broadcasted_iota(jnp.int32, (C, C), 1)
    rk = lax.broadcasted_iota(jnp.int32, (C, GLA_DK), 0)
    level_mask = [(ri // (2 * s)) == (ci // (2 * s)) for s in GLA_LEVELS]
    right_rows = [((rk // s) & 1) == 1 for s in GLA_LEVELS]
    diag_mask = ((ri // GLA_DIAG) == (ci // GLA_DIAG)) & (ci <= ri)
    groups = C // GLA_DIAG
    dstack = dstack_ref[...]
    rsel = rsel_ref[...]
    gng = gng_ref[...]

    def group_bcast(a, j):
        a3 = a.reshape(groups, GLA_DIAG, GLA_DK)
        return jnp.broadcast_to(a3[:, j:j + 1, :], (groups, GLA_DIAG, GLA_DK)).reshape(C, GLA_DK)

    def chunk(c, carry):
        rows = pl.ds(pl.multiple_of(c * C, C), C)
        for h in range(GLA_HEADS):
            sk = slice(h * GLA_DK, (h + 1) * GLA_DK)
            sv = slice(h * GLA_DV, (h + 1) * GLA_DV)
            q = q_ref[rows, sk].astype(F32)
            k = k_ref[rows, sk].astype(F32)
            v = v_ref[rows, sv]
            hi, lo = _split_hi_lo(la_ref[rows, sk])
            e2 = jnp.dot(dstack, jnp.concatenate([hi, lo], axis=1), preferred_element_type=F32)
            e_all = e2[:, :GLA_DK] + e2[:, GLA_DK:]
            b = e_all[0:C]

            a = jnp.zeros((C, C), F32)
            for li in range(len(GLA_LEVELS)):
                x = jnp.exp(-jnp.abs(e_all[(li + 1) * C:(li + 2) * C]))
                ql = jnp.where(right_rows[li], q * x, 0.0).astype(_ACT)
                kl = jnp.where(right_rows[li], 0.0, k * x).astype(_ACT)
                al = lax.dot_general(ql, kl, (((1,), (1,)), ((), ())), preferred_element_type=F32)
                a = jnp.where(level_mask[li], al, a)
            ps = []
            for j in range(GLA_DIAG):
                dec = jnp.exp(jnp.minimum(b - group_bcast(b, j), 0.0))
                ps.append((q * group_bcast(k, j) * dec).astype(_ACT))
            ad = jnp.dot(jnp.concatenate(ps, axis=1), rsel, preferred_element_type=F32)
            a = jnp.where(diag_mask, ad, a)

            st = st_ref[h]
            qe = (q * jnp.exp(b)).astype(_ACT)
            o = lax.dot_general(qe, st.astype(_ACT), (((1,), (1,)), ((), ())), preferred_element_type=F32)
            o = o + jnp.dot(a.astype(_ACT), v, preferred_element_type=F32)
            b_last = b[C - 1:C, :]
            kd = (k * jnp.exp(b_last - b)).astype(_ACT)
            upd = lax.dot_general(v, kd, (((0,), (0,)), ((), ())), preferred_element_type=F32)
            st_ref[h] = st * jnp.exp(b_last) + upd

            ms = jnp.mean(o * o, axis=-1, keepdims=True)
            y = o * lax.rsqrt(ms + GLA_NORM_EPS) * gng
            o_ref[rows, sv] = (y * _silu(g_ref[rows, sv].astype(F32))).astype(_ACT)
        return carry

    lax.fori_loop(0, R // C, chunk, 0)


def _gla(gq, gk, gv, gg, la, gng, B, S, R):
    T = gq.shape[0]
    ns = S // R
    dstack, rsel = _gla_constants()
    dstack = jnp.asarray(dstack, _ACT)
    rsel = jnp.asarray(rsel, _ACT)
    row = lambda b, i: (b * ns + i, 0)
    const = lambda b, i: (0, 0)
    rows = lambda w: pl.BlockSpec((R, w), row)
    whole = lambda a: pl.BlockSpec(a.shape, const)
    return pl.pallas_call(
        _gla_kernel,
        grid=(B, ns),
        in_specs=[rows(GLA_KW), rows(GLA_KW), rows(GLA_W), rows(GLA_W), rows(GLA_KW),
                  whole(dstack), whole(rsel), whole(gng)],
        out_specs=rows(GLA_W),
        out_shape=jax.ShapeDtypeStruct((T, GLA_W), _ACT),
        scratch_shapes=[pltpu.VMEM((GLA_HEADS, GLA_DV, GLA_DK), F32)],
        compiler_params=pltpu.CompilerParams(
            dimension_semantics=("arbitrary", "arbitrary"), vmem_limit_bytes=_VMEM_LIMIT),
        name="gla",
    )(gq, gk, gv, gg, la, dstack, rsel, gng)


def _diff_kernel(qT_ref, k_ref, vT_ref, g_ref, lq1_ref, lk1_ref, lq2_ref, lk2_ref, gcol_ref, o_ref,
                 s_ref, bm_ref, m_ref, acc_ref, *, lam_init):
    tq = qT_ref.shape[1]
    qi = pl.program_id(2)
    qT = qT_ref[...]
    comp_row = lax.broadcasted_iota(jnp.int32, qT.shape, 0) < DIFF_DQK
    zero = jnp.zeros_like(qT)
    qcs = (jnp.where(comp_row, qT, zero), jnp.where(comp_row, zero, qT))
    causal = (lax.broadcasted_iota(jnp.int32, (tq, tq), 0)
              <= lax.broadcasted_iota(jnp.int32, (tq, tq), 1))
    acc_ref[...] = jnp.zeros_like(acc_ref)
    m_ref[...] = jnp.full_like(m_ref, -jnp.inf)

    def produce(ki, slot):
        kb = k_ref[pl.ds(pl.multiple_of(ki * tq, tq), tq), :]
        for c in range(2):
            s = jnp.dot(kb, qcs[c], preferred_element_type=F32)
            s_ref[slot, c] = s
            bm_ref[slot, c] = jnp.max(s, axis=0, keepdims=True)

    def consume(ki, slot, masked):
        vb = vT_ref[:, pl.ds(pl.multiple_of(ki * tq, tq), tq)]
        for c in range(2):
            s = s_ref[slot, c]
            if masked:
                s = jnp.where(causal, s, -jnp.inf)
                bm = jnp.max(s, axis=0, keepdims=True)
            else:
                bm = bm_ref[slot, c]
            m_old = m_ref[c]
            m_new = jnp.maximum(m_old, bm)
            p = jnp.exp2(s - m_new).astype(_ACT)
            acc_ref[c] = jnp.exp2(m_old - m_new) * acc_ref[c] + jnp.dot(vb, p, preferred_element_type=F32)
            m_ref[c] = m_new

    produce(0, 0)

    def pair(j, carry):
        produce(2 * j + 1, 1)
        consume(2 * j, 0, False)
        produce(2 * j + 2, 0)
        consume(2 * j + 1, 1, False)
        return carry

    lax.fori_loop(0, qi // 2, pair, 0)

    @pl.when(qi % 2 == 1)
    def _():
        produce(qi, 1)
        consume(qi - 1, 0, False)
        consume(qi, 1, True)

    @pl.when(qi % 2 == 0)
    def _():
        consume(qi, 0, True)

    lam = (jnp.exp(jnp.sum(lq1_ref[...] * lk1_ref[...], axis=-1, keepdims=True))
           - jnp.exp(jnp.sum(lq2_ref[...] * lk2_ref[...], axis=-1, keepdims=True)) + lam_init)
    l1 = acc_ref[0, DIFF_DV:DIFF_DV + 1, :]
    l2 = acc_ref[1, DIFF_DV:DIFF_DV + 1, :]
    oT = acc_ref[0, 0:DIFF_DV, :] / l1 - lam * (acc_ref[1, 0:DIFF_DV, :] / l2)
    ms = jnp.mean(oT * oT, axis=0, keepdims=True)
    yT = oT * lax.rsqrt(ms + DIFF_NORM_EPS) * (gcol_ref[...] * (1.0 - lam_init))
    o_ref[...] = (yT.T * _silu(g_ref[...].astype(F32))).astype(_ACT)


def _diff(dqT, dk, dvT, dg, lq1, lk1, lq2, lk2, gcol, B, S, tq, lam_init):
    T = dk.shape[0]
    nq = S // tq
    small = lambda a: pl.BlockSpec(a.shape, lambda b, h, i: (0, 0))
    blk = pl.BlockSpec((tq, DIFF_DV), lambda b, h, i: (b * nq + i, h))
    return pl.pallas_call(
        functools.partial(_diff_kernel, lam_init=lam_init),
        grid=(B, DIFF_HEADS, nq),
        in_specs=[pl.BlockSpec((None, DIFF_DV, tq), lambda b, h, i: (b, h, i)),
                  pl.BlockSpec((S, DIFF_DV), lambda b, h, i: (b, h)),
                  pl.BlockSpec((None, DIFF_VT_ROWS, S), lambda b, h, i: (b, h, 0)),
                  blk, small(lq1), small(lk1), small(lq2), small(lk2), small(gcol)],
        out_specs=blk,
        out_shape=jax.ShapeDtypeStruct((T, DIFF_W), _ACT),
        scratch_shapes=[pltpu.VMEM((2, 2, tq, tq), F32), pltpu.VMEM((2, 2, 1, tq), F32),
                        pltpu.VMEM((2, 1, tq), F32), pltpu.VMEM((2, DIFF_VT_ROWS, tq), F32)],
        compiler_params=pltpu.CompilerParams(
            dimension_semantics=("arbitrary", "arbitrary", "arbitrary"), vmem_limit_bytes=_VMEM_LIMIT),
        name="diffattn",
    )(dqT, dk, dvT, dg, lq1, lk1, lq2, lk2, gcol)


def _outproj_kernel(gla_ref, dif_ref, xat_ref, x_ref, w_ref, g_ref, b_ref, o_ref, *, alpha):
    y = alpha * x_ref[...]
    y = y + jnp.dot(gla_ref[...], w_ref[0:GLA_W, :], preferred_element_type=F32)
    y = y + jnp.dot(dif_ref[...], w_ref[GLA_W:GLA_W + DIFF_W, :], preferred_element_type=F32)
    y = y + jnp.dot(xat_ref[...], w_ref[GLA_W + DIFF_W:GLA_W + DIFF_W + MEM_W, :], preferred_element_type=F32)
    mu = jnp.mean(y, axis=-1, keepdims=True)
    yc = y - mu
    var = jnp.mean(yc * yc, axis=-1, keepdims=True)
    o_ref[...] = yc * lax.rsqrt(var + LN_EPS) * g_ref[...] + b_ref[...]


def _outproj(gla, dif, xat, x2d, w_out, ln_g, ln_b, tm, alpha):
    T, D = x2d.shape
    row = lambda i: (i, 0)
    const = lambda i: (0, 0)
    rows = lambda w: pl.BlockSpec((tm, w), row)
    return pl.pallas_call(
        functools.partial(_outproj_kernel, alpha=alpha),
        grid=(T // tm,),
        in_specs=[rows(GLA_W), rows(DIFF_W), rows(MEM_W), rows(D),
                  pl.BlockSpec(w_out.shape, const, pipeline_mode=pl.Buffered(1)),
                  pl.BlockSpec(ln_g.shape, const), pl.BlockSpec(ln_b.shape, const)],
        out_specs=rows(D),
        out_shape=jax.ShapeDtypeStruct((T, D), F32),
        compiler_params=pltpu.CompilerParams(
            dimension_semantics=("arbitrary",), vmem_limit_bytes=_VMEM_LIMIT),
        name="outproj",
    )(gla, dif, xat, x2d, w_out, ln_g, ln_b)


def _rotary_tables(positions):
    B, S = positions.shape
    half = ROT_DIM // 2
    inv_freq = jnp.power(jnp.float32(ROPE_THETA), -(jnp.arange(0, ROT_DIM, 2, dtype=F32) / ROT_DIM))
    ang = positions.astype(F32)[..., None] * inv_freq
    pad = DIFF_DQK - ROT_DIM
    cos = jnp.concatenate([jnp.cos(ang)] * 2 + [jnp.ones((B, S, pad), F32)], axis=-1)
    sin = jnp.concatenate([jnp.sin(ang)] * 2 + [jnp.zeros((B, S, pad), F32)], axis=-1)
    rep = LANES // DIFF_DQK
    return (jnp.tile(cos, (1, 1, rep)).reshape(B * S, LANES),
            jnp.tile(sin, (1, 1, rep)).reshape(B * S, LANES))


def _tile(default, n):
    t = min(default, n)
    assert n % t == 0, (n, t)
    return t


def kernel(x, mem, positions, w_in, w_gk_up, b_gk_up, gla_norm_g, lambda_q1, lambda_k1, lambda_q2,
           lambda_k2, diff_norm_g, w_mem_kv, w_out, ln_g, ln_b):
    B, S, D = x.shape
    M = mem.shape[1]
    depth = w_in.shape[0]
    T = B * S
    assert S % GLA_CHUNK == 0 and M % LANES == 0
    tm_in, r_gla, tq, tm_out = _tile(_TM_IN, S), _tile(_R_GLA, S), _tile(_TQ, S), _tile(_TM_OUT, S)
    alpha = (2.0 * depth) ** 0.25

    cos_t, sin_t = _rotary_tables(positions)
    mem2d = mem.reshape(B * M, D)
    h = x.reshape(T, D)

    o_lr = 2 * GLA_KW + 2 * GLA_W
    o_d = o_lr + GLA_LOWRANK
    o_m = o_d + 4 * DIFF_W
    for l in range(depth):
        w = w_in[l]
        wa = w[:, :o_lr].astype(_ACT)
        wlr = jnp.pad(w[:, o_lr:o_d], ((0, 0), (0, LANES - GLA_LOWRANK))).astype(_ACT)
        wd = w[:, o_d:o_m].astype(_ACT)
        wm = w[:, o_m:].astype(_ACT)
        wgk = jnp.pad(w_gk_up[l].astype(F32), ((0, LANES - GLA_LOWRANK), (0, 0)))
        wgk_hi, wgk_lo = _split_hi_lo(wgk)
        wgk3 = jnp.concatenate([wgk_hi, wgk_lo, wgk_hi], axis=0)
        bgk = b_gk_up[l].astype(F32).reshape(1, GLA_KW)

        mkT, mv = _memkv(mem2d, w_mem_kv[l].astype(_ACT), B, M)
        gq, gk, gv, gg, la, dqT, dk, dvT, dg, xat = _inproj(
            h, wa, wlr, wd, wm, wgk3, bgk, cos_t, sin_t, mkT, mv, B, S, M, tm_in)
        gla = _gla(gq, gk, gv, gg, la, gla_norm_g[l].astype(F32).reshape(1, GLA_DV), B, S, r_gla)
        row64 = lambda a: a[l].astype(F32).reshape(1, DIFF_DQK)
        dif = _diff(dqT, dk, dvT, dg, row64(lambda_q1), row64(lambda_k1), row64(lambda_q2),
                    row64(lambda_k2), diff_norm_g[l].astype(F32).reshape(DIFF_DV, 1), B, S, tq,
                    _lambda_init(l))
        h = _outproj(gla, dif, xat, h, w_out[l].astype(_ACT), ln_g[l].astype(F32).reshape(1, D),
                     ln_b[l].astype(F32).reshape(1, D), tm_out, alpha)
    return h.reshape(B, S, D)
```

```python
import functools
import math

import jax
import jax.numpy as jnp
import numpy as np
from jax import lax
from jax.experimental import pallas as pl
from jax.experimental.pallas import tpu as pltpu

F32 = jnp.float32
_ACT = jnp.bfloat16

GLA_HEADS = 4
GLA_DK = 128
GLA_DV = 256
GLA_LOWRANK = 16
GLA_TAU = 16.0
GLA_NORM_EPS = 1e-6
GLA_W = GLA_HEADS * GLA_DV
GLA_KW = GLA_HEADS * GLA_DK

DIFF_HEADS = 4
DIFF_DV = 128
DIFF_DQK = 64
DIFF_NORM_EPS = 1e-5
DIFF_W = DIFF_HEADS * DIFF_DV
ACT_SUBLANES = 16
DIFF_VT_ROWS = DIFF_DV + ACT_SUBLANES
LOG2E = math.log2(math.e)

MEM_HEADS = 4
MEM_DH = 128
MEM_W = MEM_HEADS * MEM_DH

ROPE_THETA = 500000.0
ROT_DIM = DIFF_DQK // 4
LN_EPS = 1e-5

LANES = 128
SUBLANES = 8

W_OFF_GLA = 0
W_OFF_DIFF = W_OFF_GLA + 2 * GLA_KW + 2 * GLA_W
W_OFF_MEM = W_OFF_DIFF + 4 * DIFF_W
W_OFF_LR = W_OFF_MEM + 2 * MEM_W
W_COLS = W_OFF_LR + LANES

GLA_CHUNK = 64
GLA_LEVELS = (32, 16, 8)
GLA_DIAG = 8

_TM_IN = 256
_R_GLA = 512
_TQ = 512
_TM_OUT = 512
_VMEM_LIMIT = 60 * 1024 * 1024


def _lambda_init(layer):
    return 0.8 - 0.6 * math.exp(-0.3 * layer)


def _silu(g):
    return g / (1.0 + jnp.exp(-g))


def _split_hi_lo(a):
    hi = a.astype(_ACT)
    lo = (a - hi.astype(F32)).astype(_ACT)
    return hi, lo


def _memkv_kernel(mem_ref, w_ref, mkT_ref, mv_ref):
    kv = jnp.dot(mem_ref[...].astype(_ACT), w_ref[...], preferred_element_type=F32)
    mkT_ref[...] = kv[:, :MEM_W].T.astype(_ACT)
    mv_ref[...] = kv[:, MEM_W:].astype(_ACT)


def _memkv(mem2d, w_kv, B, M):
    D = mem2d.shape[1]
    return pl.pallas_call(
        _memkv_kernel,
        grid=(B,),
        in_specs=[pl.BlockSpec((M, D), lambda b: (b, 0)),
                  pl.BlockSpec((D, 2 * MEM_W), lambda b: (0, 0))],
        out_specs=[pl.BlockSpec((None, MEM_W, M), lambda b: (b, 0, 0)),
                   pl.BlockSpec((M, MEM_W), lambda b: (b, 0))],
        out_shape=[jax.ShapeDtypeStruct((B, MEM_W, M), _ACT),
                   jax.ShapeDtypeStruct((B * M, MEM_W), _ACT)],
        compiler_params=pltpu.CompilerParams(vmem_limit_bytes=_VMEM_LIMIT),
        name="memkv",
    )(mem2d, w_kv)


def _inproj_kernel(x_ref, w_ref, wgk_ref, bgk_ref, cos_ref, sin_ref, mkT_ref, mv_ref,
                   gq_ref, gk_ref, gv_ref, gg_ref, la_ref, dqT_ref, dk_ref, dvT_ref, dg_ref, xat_ref):
    tm = x_ref.shape[0]
    xb = x_ref[...].astype(_ACT)

    def proj(base, c0, c1):
        return jnp.dot(xb, w_ref[:, base + c0:base + c1], preferred_element_type=F32)

    gq_ref[...] = (proj(W_OFF_GLA,0, GLA_KW) * (GLA_DK ** -0.5)).astype(_ACT)
    gk_ref[...] = proj(W_OFF_GLA,GLA_KW, 2 * GLA_KW).astype(_ACT)
    gv_ref[...] = proj(W_OFF_GLA,2 * GLA_KW, 2 * GLA_KW + GLA_W).astype(_ACT)
    gg_ref[...] = proj(W_OFF_GLA,2 * GLA_KW + GLA_W, 2 * GLA_KW + 2 * GLA_W).astype(_ACT)

    glr = proj(W_OFF_LR,0, LANES)
    hi, lo = _split_hi_lo(glr)
    logit = jnp.dot(jnp.concatenate([hi, hi, lo], axis=1), wgk_ref[...],
                    preferred_element_type=F32) + bgk_ref[...]
    log_sig = jnp.minimum(logit, 0.0) - jnp.log(1.0 + jnp.exp(-jnp.abs(logit)))
    la_ref[...] = log_sig * (LOG2E / GLA_TAU)

    lane = lax.broadcasted_iota(jnp.int32, (tm, LANES), 1) & (DIFF_DQK - 1)
    cosv = cos_ref[...]
    sinv = sin_ref[...]
    half = ROT_DIM // 2

    def rope(a):
        up = pltpu.roll(a, LANES - half, 1)
        dn = pltpu.roll(a, half, 1)
        return a * cosv + jnp.where(lane < half, -up, dn) * sinv

    dq = proj(W_OFF_DIFF,0, DIFF_W)
    dk = proj(W_OFF_DIFF,DIFF_W, 2 * DIFF_W)
    dv = proj(W_OFF_DIFF,2 * DIFF_W, 3 * DIFF_W)
    ones = jnp.ones((ACT_SUBLANES, tm), _ACT)
    for h in range(DIFF_HEADS):
        sl = slice(h * DIFF_DV, (h + 1) * DIFF_DV)
        dqT_ref[sl, :] = (rope(dq[:, sl]) * (DIFF_DQK ** -0.5 * LOG2E)).T.astype(_ACT)
        dk_ref[:, sl] = rope(dk[:, sl]).astype(_ACT)
        dvT_ref[h * DIFF_VT_ROWS:h * DIFF_VT_ROWS + DIFF_DV, :] = dv[:, sl].T.astype(_ACT)
        dvT_ref[h * DIFF_VT_ROWS + DIFF_DV:(h + 1) * DIFF_VT_ROWS, :] = ones
    dg_ref[...] = proj(W_OFF_DIFF,3 * DIFF_W, 4 * DIFF_W).astype(_ACT)

    mq = proj(W_OFF_MEM,0, MEM_W)
    mg = proj(W_OFF_MEM,MEM_W, 2 * MEM_W)
    for h in range(MEM_HEADS):
        sl = slice(h * MEM_DH, (h + 1) * MEM_DH)
        qh = (mq[:, sl] * (MEM_DH ** -0.5)).astype(_ACT)
        s = jnp.dot(qh, mkT_ref[sl, :], preferred_element_type=F32)
        p = jnp.exp(s - jnp.max(s, axis=-1, keepdims=True))
        l = jnp.sum(p, axis=-1, keepdims=True)
        o = jnp.dot(p.astype(_ACT), mv_ref[:, sl], preferred_element_type=F32) / l
        xat_ref[:, sl] = (o * _silu(mg[:, sl])).astype(_ACT)


def _inproj(x2d, w_all, wgk3, bgk, cos_t, sin_t, mkT, mv, B, S, M, tm):
    T, D = x2d.shape
    ns = S // tm
    row = lambda b, i: (b * ns + i, 0)
    const = lambda b, i: (0, 0)
    whole = lambda a: pl.BlockSpec(a.shape, const, pipeline_mode=pl.Buffered(1))
    rows = lambda w: pl.BlockSpec((tm, w), row)
    tr = pl.BlockSpec((None, DIFF_W, tm), lambda b, i: (b, 0, i))
    trv = pl.BlockSpec((None, DIFF_HEADS * DIFF_VT_ROWS, tm), lambda b, i: (b, 0, i))
    return pl.pallas_call(
        _inproj_kernel,
        grid=(B, ns),
        in_specs=[rows(D), whole(w_all), whole(wgk3), whole(bgk),
                  rows(LANES), rows(LANES),
                  pl.BlockSpec((None, MEM_W, M), lambda b, i: (b, 0, 0)),
                  pl.BlockSpec((M, MEM_W), lambda b, i: (b, 0))],
        out_specs=[rows(GLA_KW), rows(GLA_KW), rows(GLA_W), rows(GLA_W), rows(GLA_KW),
                   tr, rows(DIFF_W), trv, rows(DIFF_W), rows(MEM_W)],
        out_shape=[jax.ShapeDtypeStruct((T, GLA_KW), _ACT), jax.ShapeDtypeStruct((T, GLA_KW), _ACT),
                   jax.ShapeDtypeStruct((T, GLA_W), _ACT), jax.ShapeDtypeStruct((T, GLA_W), _ACT),
                   jax.ShapeDtypeStruct((T, GLA_KW), F32),
                   jax.ShapeDtypeStruct((B, DIFF_W, S), _ACT), jax.ShapeDtypeStruct((T, DIFF_W), _ACT),
                   jax.ShapeDtypeStruct((B, DIFF_HEADS * DIFF_VT_ROWS, S), _ACT),
                   jax.ShapeDtypeStruct((T, DIFF_W), _ACT),
                   jax.ShapeDtypeStruct((T, MEM_W), _ACT)],
        compiler_params=pltpu.CompilerParams(
            dimension_semantics=("arbitrary", "arbitrary"), vmem_limit_bytes=_VMEM_LIMIT),
        name="inproj",
    )(x2d, w_all, wgk3, bgk, cos_t, sin_t, mkT, mv)


def _gla_constants():
    C = GLA_CHUNK
    t = np.arange(C)
    tril = (t[None, :] <= t[:, None]).astype(np.float32)
    rsel = np.zeros((GLA_DIAG * GLA_DK, C), np.float32)
    for j in range(GLA_DIAG):
        rsel[j * GLA_DK:(j + 1) * GLA_DK, j::GLA_DIAG] = 1.0
    return tril, rsel


def _gla_kernel(q_ref, k_ref, v_ref, g_ref, la_ref, tril_ref, rsel_ref, gng_ref, o_ref, st_ref):
    R = q_ref.shape[0]
    C = GLA_CHUNK

    @pl.when(pl.program_id(1) == 0)
    def _():
        st_ref[...] = jnp.zeros_like(st_ref)

    ri = lax.broadcasted_iota(jnp.int32, (C, C), 0)
    ci = lax.broadcasted_iota(jnp.int32, (C, C), 1)
    rk = lax.broadcasted_iota(jnp.int32, (C, GLA_KW), 0)
    pair_mask = [((ri // (2 * s)) == (ci // (2 * s))) & (((ri // s) & 1) == 1) & (((ci // s) & 1) == 0)
                 for s in GLA_LEVELS]
    half_sign = [jnp.where(((rk // s) & 1) == 1, 1.0, -1.0) for s in GLA_LEVELS]
    diag_mask = ((ri // GLA_DIAG) == (ci // GLA_DIAG)) & (ci <= ri)
    groups = C // GLA_DIAG
    tril = tril_ref[...]
    rsel = rsel_ref[...]
    gng = gng_ref[...]

    heads = range(GLA_HEADS)
    sk = [slice(h * GLA_DK, (h + 1) * GLA_DK) for h in heads]
    sv = [slice(h * GLA_DV, (h + 1) * GLA_DV) for h in heads]
    nt = (((1,), (1,)), ((), ()))

    def group_bcast(a, j):
        a3 = a.reshape(groups, GLA_DIAG, GLA_KW)
        return jnp.broadcast_to(a3[:, j:j + 1, :], (groups, GLA_DIAG, GLA_KW)).reshape(C, GLA_KW)

    def chunk(c, carry):
        rows = pl.ds(pl.multiple_of(c * C, C), C)
        q = q_ref[rows, :].astype(F32)
        k = k_ref[rows, :].astype(F32)
        hi, lo = _split_hi_lo(la_ref[rows, :])
        cs = jnp.dot(tril, jnp.concatenate([hi, lo], axis=1), preferred_element_type=F32)
        b = cs[:, :GLA_KW] + cs[:, GLA_KW:]

        a = [jnp.zeros((C, C), F32) for _ in heads]
        for li, s in enumerate(GLA_LEVELS):
            ref = jnp.concatenate(
                [jnp.broadcast_to(b[p + s - 1:p + s, :], (2 * s, GLA_KW)) for p in range(0, C, 2 * s)],
                axis=0)
            x = jnp.exp2(half_sign[li] * (b - ref))
            qx = (q * x).astype(_ACT)
            kx = (k * x).astype(_ACT)
            for h in heads:
                al = lax.dot_general(qx[:, sk[h]], kx[:, sk[h]], nt, preferred_element_type=F32)
                a[h] = jnp.where(pair_mask[li], al, a[h])
        ps = []
        for j in range(GLA_DIAG):
            dec = jnp.exp2(jnp.minimum(b - group_bcast(b, j), 0.0))
            ps.append((q * group_bcast(k, j) * dec).astype(_ACT))
        for h in heads:
            ad = jnp.dot(jnp.concatenate([p[:, sk[h]] for p in ps], axis=1), rsel,
                         preferred_element_type=F32)
            a[h] = jnp.where(diag_mask, ad, a[h])

        qe = (q * jnp.exp2(b)).astype(_ACT)
        b_last = b[C - 1:C, :]
        kd = (k * jnp.exp2(b_last - b)).astype(_ACT)
        keep = jnp.exp2(b_last)
        for h in heads:
            v = v_ref[rows, sv[h]]
            st = st_ref[h]
            o = lax.dot_general(qe[:, sk[h]], st.astype(_ACT), nt, preferred_element_type=F32)
            o = o + jnp.dot(a[h].astype(_ACT), v, preferred_element_type=F32)
            upd = lax.dot_general(v, kd[:, sk[h]], (((0,), (0,)), ((), ())), preferred_element_type=F32)
            st_ref[h] = st * keep[:, sk[h]] + upd

            ms = jnp.mean(o * o, axis=-1, keepdims=True)
            y = o * lax.rsqrt(ms + GLA_NORM_EPS) * gng
            o_ref[rows, sv[h]] = (y * _silu(g_ref[rows, sv[h]].astype(F32))).astype(_ACT)
        return carry

    lax.fori_loop(0, R // C, chunk, 0, unroll=2)


def _gla(gq, gk, gv, gg, la, gng, B, S, R):
    T = gq.shape[0]
    ns = S // R
    tril, rsel = _gla_constants()
    tril = jnp.asarray(tril, _ACT)
    rsel = jnp.asarray(rsel, _ACT)
    row = lambda b, i: (b * ns + i, 0)
    const = lambda b, i: (0, 0)
    rows = lambda w: pl.BlockSpec((R, w), row)
    whole = lambda a: pl.BlockSpec(a.shape, const)
    return pl.pallas_call(
        _gla_kernel,
        grid=(B, ns),
        in_specs=[rows(GLA_KW), rows(GLA_KW), rows(GLA_W), rows(GLA_W), rows(GLA_KW),
                  whole(tril), whole(rsel), whole(gng)],
        out_specs=rows(GLA_W),
        out_shape=jax.ShapeDtypeStruct((T, GLA_W), _ACT),
        scratch_shapes=[pltpu.VMEM((GLA_HEADS, GLA_DV, GLA_DK), F32)],
        compiler_params=pltpu.CompilerParams(
            dimension_semantics=("arbitrary", "arbitrary"), vmem_limit_bytes=_VMEM_LIMIT),
        name="gla",
    )(gq, gk, gv, gg, la, tril, rsel, gng)


def _diff_kernel(qT_ref, k_ref, vT_ref, g_ref, lq1_ref, lk1_ref, lq2_ref, lk2_ref, gcol_ref, o_ref,
                 s_ref, bm_ref, m_ref, acc_ref, *, lam_init):
    tq = qT_ref.shape[1]
    qi = pl.program_id(2)
    qT = qT_ref[...]
    comp_row = lax.broadcasted_iota(jnp.int32, qT.shape, 0) < DIFF_DQK
    zero = jnp.zeros_like(qT)
    qcs = (jnp.where(comp_row, qT, zero), jnp.where(comp_row, zero, qT))
    causal = (lax.broadcasted_iota(jnp.int32, (tq, tq), 0)
              <= lax.broadcasted_iota(jnp.int32, (tq, tq), 1))
    acc_ref[...] = jnp.zeros_like(acc_ref)
    m_ref[...] = jnp.full_like(m_ref, -jnp.inf)

    def produce(ki, slot):
        kb = k_ref[pl.ds(pl.multiple_of(ki * tq, tq), tq), :]
        for c in range(2):
            s = jnp.dot(kb, qcs[c], preferred_element_type=F32)
            s_ref[slot, c] = s
            bm_ref[slot, c] = jnp.max(s, axis=0, keepdims=True)

    def consume(ki, slot, masked):
        vb = vT_ref[:, pl.ds(pl.multiple_of(ki * tq, tq), tq)]
        for c in range(2):
            s = s_ref[slot, c]
            if masked:
                s = jnp.where(causal, s, -jnp.inf)
                bm = jnp.max(s, axis=0, keepdims=True)
            else:
                bm = bm_ref[slot, c]
            m_old = m_ref[c]
            m_new = jnp.maximum(m_old, bm)
            p = jnp.exp2(s - m_new).astype(_ACT)
            acc_ref[c] = jnp.exp2(m_old - m_new) * acc_ref[c] + jnp.dot(vb, p, preferred_element_type=F32)
            m_ref[c] = m_new

    produce(0, 0)

    def pair(j, carry):
        produce(2 * j + 1, 1)
        consume(2 * j, 0, False)
        produce(2 * j + 2, 0)
        consume(2 * j + 1, 1, False)
        return carry

    lax.fori_loop(0, qi // 2, pair, 0)

    @pl.when(qi % 2 == 1)
    def _():
        produce(qi, 1)
        consume(qi - 1, 0, False)
        consume(qi, 1, True)

    @pl.when(qi % 2 == 0)
    def _():
        consume(qi, 0, True)

    lam = (jnp.exp(jnp.sum(lq1_ref[...] * lk1_ref[...], axis=-1, keepdims=True))
           - jnp.exp(jnp.sum(lq2_ref[...] * lk2_ref[...], axis=-1, keepdims=True)) + lam_init)
    l1 = acc_ref[0, DIFF_DV:DIFF_DV + 1, :]
    l2 = acc_ref[1, DIFF_DV:DIFF_DV + 1, :]
    oT = acc_ref[0, 0:DIFF_DV, :] / l1 - lam * (acc_ref[1, 0:DIFF_DV, :] / l2)
    ms = jnp.mean(oT * oT, axis=0, keepdims=True)
    yT = oT * lax.rsqrt(ms + DIFF_NORM_EPS) * (gcol_ref[...] * (1.0 - lam_init))
    o_ref[...] = (yT.T * _silu(g_ref[...].astype(F32))).astype(_ACT)


def _diff(dqT, dk, dvT, dg, lq1, lk1, lq2, lk2, gcol, B, S, tq, lam_init):
    T = dk.shape[0]
    nq = S // tq
    small = lambda a: pl.BlockSpec(a.shape, lambda b, h, i: (0, 0))
    blk = pl.BlockSpec((tq, DIFF_DV), lambda b, h, i: (b * nq + i, h))
    return pl.pallas_call(
        functools.partial(_diff_kernel, lam_init=lam_init),
        grid=(B, DIFF_HEADS, nq),
        in_specs=[pl.BlockSpec((None, DIFF_DV, tq), lambda b, h, i: (b, h, i)),
                  pl.BlockSpec((S, DIFF_DV), lambda b, h, i: (b, h)),
                  pl.BlockSpec((None, DIFF_VT_ROWS, S), lambda b, h, i: (b, h, 0)),
                  blk, small(lq1), small(lk1), small(lq2), small(lk2), small(gcol)],
        out_specs=blk,
        out_shape=jax.ShapeDtypeStruct((T, DIFF_W), _ACT),
        scratch_shapes=[pltpu.VMEM((2, 2, tq, tq), F32), pltpu.VMEM((2, 2, 1, tq), F32),
                        pltpu.VMEM((2, 1, tq), F32), pltpu.VMEM((2, DIFF_VT_ROWS, tq), F32)],
        compiler_params=pltpu.CompilerParams(
            dimension_semantics=("arbitrary", "arbitrary", "arbitrary"), vmem_limit_bytes=_VMEM_LIMIT),
        name="diffattn",
    )(dqT, dk, dvT, dg, lq1, lk1, lq2, lk2, gcol)


def _outproj_kernel(gla_ref, dif_ref, xat_ref, x_ref, w_ref, g_ref, b_ref, o_ref, *, alpha):
    y = alpha * x_ref[...]
    y = y + jnp.dot(gla_ref[...], w_ref[0:GLA_W, :], preferred_element_type=F32)
    y = y + jnp.dot(dif_ref[...], w_ref[GLA_W:GLA_W + DIFF_W, :], preferred_element_type=F32)
    y = y + jnp.dot(xat_ref[...], w_ref[GLA_W + DIFF_W:GLA_W + DIFF_W + MEM_W, :], preferred_element_type=F32)
    mu = jnp.mean(y, axis=-1, keepdims=True)
    yc = y - mu
    var = jnp.mean(yc * yc, axis=-1, keepdims=True)
    o_ref[...] = yc * lax.rsqrt(var + LN_EPS) * g_ref[...] + b_ref[...]


def _outproj(gla, dif, xat, x2d, w_out, ln_g, ln_b, tm, alpha):
    T, D = x2d.shape
    row = lambda i: (i, 0)
    const = lambda i: (0, 0)
    rows = lambda w: pl.BlockSpec((tm, w), row)
    return pl.pallas_call(
        functools.partial(_outproj_kernel, alpha=alpha),
        grid=(T // tm,),
        in_specs=[rows(GLA_W), rows(DIFF_W), rows(MEM_W), rows(D),
                  pl.BlockSpec(w_out.shape, const, pipeline_mode=pl.Buffered(1)),
                  pl.BlockSpec(ln_g.shape, const), pl.BlockSpec(ln_b.shape, const)],
        out_specs=rows(D),
        out_shape=jax.ShapeDtypeStruct((T, D), F32),
        compiler_params=pltpu.CompilerParams(
            dimension_semantics=("arbitrary",), vmem_limit_bytes=_VMEM_LIMIT),
        name="outproj",
    )(gla, dif, xat, x2d, w_out, ln_g, ln_b)


def _rotary_tables(positions):
    B, S = positions.shape
    half = ROT_DIM // 2
    inv_freq = jnp.power(jnp.float32(ROPE_THETA), -(jnp.arange(0, ROT_DIM, 2, dtype=F32) / ROT_DIM))
    lane = np.arange(LANES) % DIFF_DQK
    lane_freq = jnp.where(lane < ROT_DIM, inv_freq[lane % half], 0.0)
    ang = positions.astype(F32).reshape(B * S, 1) * lane_freq[None, :]
    return jnp.cos(ang), jnp.sin(ang)


def _tile(default, n):
    t = min(default, n)
    assert n % t == 0, (n, t)
    return t


def kernel(x, mem, positions, w_in, w_gk_up, b_gk_up, gla_norm_g, lambda_q1, lambda_k1, lambda_q2,
           lambda_k2, diff_norm_g, w_mem_kv, w_out, ln_g, ln_b):
    B, S, D = x.shape
    M = mem.shape[1]
    depth = w_in.shape[0]
    T = B * S
    assert S % GLA_CHUNK == 0 and M % LANES == 0
    tm_in, r_gla, tq, tm_out = _tile(_TM_IN, S), _tile(_R_GLA, S), _tile(_TQ, S), _tile(_TM_OUT, S)
    alpha = (2.0 * depth) ** 0.25

    cos_t, sin_t = _rotary_tables(positions)
    mem2d = mem.reshape(B * M, D)
    h = x.reshape(T, D)

    o_lr = 2 * GLA_KW + 2 * GLA_W
    o_d = o_lr + GLA_LOWRANK
    o_m = o_d + 4 * DIFF_W
    for l in range(depth):
        w = w_in[l]
        w_all = jnp.concatenate(
            [w[:, :o_lr], w[:, o_d:], w[:, o_lr:o_d], jnp.zeros((D, LANES - GLA_LOWRANK), w.dtype)],
            axis=1).astype(_ACT)
        wgk = jnp.pad(w_gk_up[l].astype(F32), ((0, LANES - GLA_LOWRANK), (0, 0)))
        wgk_hi, wgk_lo = _split_hi_lo(wgk)
        wgk3 = jnp.concatenate([wgk_hi, wgk_lo, wgk_hi], axis=0)
        bgk = b_gk_up[l].astype(F32).reshape(1, GLA_KW)

        mkT, mv = _memkv(mem2d, w_mem_kv[l].astype(_ACT), B, M)
        gq, gk, gv, gg, la, dqT, dk, dvT, dg, xat = _inproj(
            h, w_all, wgk3, bgk, cos_t, sin_t, mkT, mv, B, S, M, tm_in)
        gla = _gla(gq, gk, gv, gg, la, gla_norm_g[l].astype(F32).reshape(1, GLA_DV), B, S, r_gla)
        row64 = lambda a: a[l].astype(F32).reshape(1, DIFF_DQK)
        dif = _diff(dqT, dk, dvT, dg, row64(lambda_q1), row64(lambda_k1), row64(lambda_q2),
                    row64(lambda_k2), diff_norm_g[l].astype(F32).reshape(DIFF_DV, 1), B, S, tq,
                    _lambda_init(l))
        h = _outproj(gla, dif, xat, h, w_out[l].astype(_ACT), ln_g[l].astype(F32).reshape(1, D),
                     ln_b[l].astype(F32).reshape(1, D), tm_out, alpha)
    return h.reshape(B, S, D)
```

```python
import functools
import math

import jax
import jax.numpy as jnp
import numpy as np
from jax import lax
from jax.experimental import pallas as pl
from jax.experimental.pallas import tpu as pltpu

F32 = jnp.float32
_ACT = jnp.bfloat16

GLA_HEADS = 4
GLA_DK = 128
GLA_DV = 256
GLA_LOWRANK = 16
GLA_TAU = 16.0
GLA_NORM_EPS = 1e-6
GLA_W = GLA_HEADS * GLA_DV
GLA_KW = GLA_HEADS * GLA_DK

DIFF_HEADS = 4
DIFF_DV = 128
DIFF_DQK = 64
DIFF_NORM_EPS = 1e-5
DIFF_W = DIFF_HEADS * DIFF_DV
ACT_SUBLANES = 16
DIFF_VT_ROWS = DIFF_DV + ACT_SUBLANES
LOG2E = math.log2(math.e)

MEM_HEADS = 4
MEM_DH = 128
MEM_W = MEM_HEADS * MEM_DH

ROPE_THETA = 500000.0
ROT_DIM = DIFF_DQK // 4
LN_EPS = 1e-5

LANES = 128
SUBLANES = 8

W_OFF_GLA = 0
W_OFF_DIFF = W_OFF_GLA + 2 * GLA_KW + 2 * GLA_W
W_OFF_MEM = W_OFF_DIFF + 4 * DIFF_W
W_OFF_LR = W_OFF_MEM + 2 * MEM_W
W_COLS = W_OFF_LR + LANES

GLA_CHUNK = 64
GLA_LEVELS = (32, 16, 8)
GLA_DIAG = 8

_TM_IN = 256
_R_GLA = 512
_TQ = 1024
_TM_OUT = 512
_DIFF_LANE_WINDOW = 256
_VMEM_LIMIT = 60 * 1024 * 1024


def _lambda_init(layer):
    return 0.8 - 0.6 * math.exp(-0.3 * layer)


def _silu(g):
    return g / (1.0 + jnp.exp(-g))


def _split_hi_lo(a):
    hi = a.astype(_ACT)
    lo = (a - hi.astype(F32)).astype(_ACT)
    return hi, lo


def _memkv_kernel(mem_ref, w_ref, mkT_ref, mv_ref):
    kv = jnp.dot(mem_ref[...].astype(_ACT), w_ref[...], preferred_element_type=F32)
    mkT_ref[...] = kv[:, :MEM_W].T.astype(_ACT)
    mv_ref[...] = kv[:, MEM_W:].astype(_ACT)


def _memkv(mem2d, w_kv, B, M):
    D = mem2d.shape[1]
    return pl.pallas_call(
        _memkv_kernel,
        grid=(B,),
        in_specs=[pl.BlockSpec((M, D), lambda b: (b, 0)),
                  pl.BlockSpec((D, 2 * MEM_W), lambda b: (0, 0))],
        out_specs=[pl.BlockSpec((None, MEM_W, M), lambda b: (b, 0, 0)),
                   pl.BlockSpec((M, MEM_W), lambda b: (b, 0))],
        out_shape=[jax.ShapeDtypeStruct((B, MEM_W, M), _ACT),
                   jax.ShapeDtypeStruct((B * M, MEM_W), _ACT)],
        compiler_params=pltpu.CompilerParams(vmem_limit_bytes=_VMEM_LIMIT),
        name="memkv",
    )(mem2d, w_kv)


def _inproj_kernel(x_ref, w_ref, wgk_ref, bgk_ref, cos_ref, sin_ref, mkT_ref, mv_ref,
                   gq_ref, gk_ref, gv_ref, gg_ref, la_ref, dqT_ref, dk_ref, dvT_ref, dg_ref, xat_ref):
    tm = x_ref.shape[0]
    xb = x_ref[...].astype(_ACT)

    def proj(base, c0, c1):
        return jnp.dot(xb, w_ref[:, base + c0:base + c1], preferred_element_type=F32)

    gq_ref[...] = (proj(W_OFF_GLA,0, GLA_KW) * (GLA_DK ** -0.5)).astype(_ACT)
    gk_ref[...] = proj(W_OFF_GLA,GLA_KW, 2 * GLA_KW).astype(_ACT)
    gv_ref[...] = proj(W_OFF_GLA,2 * GLA_KW, 2 * GLA_KW + GLA_W).astype(_ACT)
    gg_ref[...] = proj(W_OFF_GLA,2 * GLA_KW + GLA_W, 2 * GLA_KW + 2 * GLA_W).astype(_ACT)

    glr = proj(W_OFF_LR,0, LANES)
    hi, lo = _split_hi_lo(glr)
    logit = jnp.dot(jnp.concatenate([hi, hi, lo], axis=1), wgk_ref[...],
                    preferred_element_type=F32) + bgk_ref[...]
    log_sig = jnp.minimum(logit, 0.0) - jnp.log(1.0 + jnp.exp(-jnp.abs(logit)))
    la_ref[...] = log_sig * (LOG2E / GLA_TAU)

    lane = lax.broadcasted_iota(jnp.int32, (tm, LANES), 1) & (DIFF_DQK - 1)
    cosv = cos_ref[...]
    sinv = sin_ref[...]
    half = ROT_DIM // 2

    def rope(a):
        up = pltpu.roll(a, LANES - half, 1)
        dn = pltpu.roll(a, half, 1)
        return a * cosv + jnp.where(lane < half, -up, dn) * sinv

    dq = proj(W_OFF_DIFF,0, DIFF_W)
    dk = proj(W_OFF_DIFF,DIFF_W, 2 * DIFF_W)
    dv = proj(W_OFF_DIFF,2 * DIFF_W, 3 * DIFF_W)
    ones = jnp.ones((ACT_SUBLANES, tm), _ACT)
    for h in range(DIFF_HEADS):
        sl = slice(h * DIFF_DV, (h + 1) * DIFF_DV)
        dqT_ref[sl, :] = (rope(dq[:, sl]) * (DIFF_DQK ** -0.5 * LOG2E)).T.astype(_ACT)
        dk_ref[:, sl] = rope(dk[:, sl]).astype(_ACT)
        dvT_ref[h * DIFF_VT_ROWS:h * DIFF_VT_ROWS + DIFF_DV, :] = dv[:, sl].T.astype(_ACT)
        dvT_ref[h * DIFF_VT_ROWS + DIFF_DV:(h + 1) * DIFF_VT_ROWS, :] = ones
    dg_ref[...] = proj(W_OFF_DIFF,3 * DIFF_W, 4 * DIFF_W).astype(_ACT)

    mq = proj(W_OFF_MEM,0, MEM_W)
    mg = proj(W_OFF_MEM,MEM_W, 2 * MEM_W)
    for h in range(MEM_HEADS):
        sl = slice(h * MEM_DH, (h + 1) * MEM_DH)
        qh = (mq[:, sl] * (MEM_DH ** -0.5)).astype(_ACT)
        s = jnp.dot(qh, mkT_ref[sl, :], preferred_element_type=F32)
        p = jnp.exp(s - jnp.max(s, axis=-1, keepdims=True))
        l = jnp.sum(p, axis=-1, keepdims=True)
        o = jnp.dot(p.astype(_ACT), mv_ref[:, sl], preferred_element_type=F32) / l
        xat_ref[:, sl] = (o * _silu(mg[:, sl])).astype(_ACT)


def _inproj(x2d, w_all, wgk3, bgk, cos_t, sin_t, mkT, mv, B, S, M, tm):
    T, D = x2d.shape
    ns = S // tm
    row = lambda b, i: (b * ns + i, 0)
    const = lambda b, i: (0, 0)
    whole = lambda a: pl.BlockSpec(a.shape, const, pipeline_mode=pl.Buffered(1))
    rows = lambda w: pl.BlockSpec((tm, w), row)
    tr = pl.BlockSpec((None, DIFF_W, tm), lambda b, i: (b, 0, i))
    trv = pl.BlockSpec((None, DIFF_HEADS * DIFF_VT_ROWS, tm), lambda b, i: (b, 0, i))
    return pl.pallas_call(
        _inproj_kernel,
        grid=(B, ns),
        in_specs=[rows(D), whole(w_all), whole(wgk3), whole(bgk),
                  rows(LANES), rows(LANES),
                  pl.BlockSpec((None, MEM_W, M), lambda b, i: (b, 0, 0)),
                  pl.BlockSpec((M, MEM_W), lambda b, i: (b, 0))],
        out_specs=[rows(GLA_KW), rows(GLA_KW), rows(GLA_W), rows(GLA_W), rows(GLA_KW),
                   tr, rows(DIFF_W), trv, rows(DIFF_W), rows(MEM_W)],
        out_shape=[jax.ShapeDtypeStruct((T, GLA_KW), _ACT), jax.ShapeDtypeStruct((T, GLA_KW), _ACT),
                   jax.ShapeDtypeStruct((T, GLA_W), _ACT), jax.ShapeDtypeStruct((T, GLA_W), _ACT),
                   jax.ShapeDtypeStruct((T, GLA_KW), F32),
                   jax.ShapeDtypeStruct((B, DIFF_W, S), _ACT), jax.ShapeDtypeStruct((T, DIFF_W), _ACT),
                   jax.ShapeDtypeStruct((B, DIFF_HEADS * DIFF_VT_ROWS, S), _ACT),
                   jax.ShapeDtypeStruct((T, DIFF_W), _ACT),
                   jax.ShapeDtypeStruct((T, MEM_W), _ACT)],
        compiler_params=pltpu.CompilerParams(
            dimension_semantics=("arbitrary", "arbitrary"), vmem_limit_bytes=_VMEM_LIMIT),
        name="inproj",
    )(x2d, w_all, wgk3, bgk, cos_t, sin_t, mkT, mv)


def _gla_constants():
    C = GLA_CHUNK
    t = np.arange(C)
    tril = (t[None, :] <= t[:, None]).astype(np.float32)
    rsel = np.zeros((GLA_DIAG * GLA_DK, C), np.float32)
    for j in range(GLA_DIAG):
        rsel[j * GLA_DK:(j + 1) * GLA_DK, j::GLA_DIAG] = 1.0
    return tril, rsel


def _gla_kernel(q_ref, k_ref, v_ref, g_ref, la_ref, tril_ref, rsel_ref, gng_ref, o_ref, st_ref):
    R = q_ref.shape[0]
    C = GLA_CHUNK

    @pl.when(pl.program_id(1) == 0)
    def _():
        st_ref[...] = jnp.zeros_like(st_ref)

    ri = lax.broadcasted_iota(jnp.int32, (C, C), 0)
    ci = lax.broadcasted_iota(jnp.int32, (C, C), 1)
    rk = lax.broadcasted_iota(jnp.int32, (C, GLA_KW), 0)
    pair_mask = [((ri // (2 * s)) == (ci // (2 * s))) & (((ri // s) & 1) == 1) & (((ci // s) & 1) == 0)
                 for s in GLA_LEVELS]
    half_sign = [jnp.where(((rk // s) & 1) == 1, 1.0, -1.0) for s in GLA_LEVELS]
    diag_mask = ((ri // GLA_DIAG) == (ci // GLA_DIAG)) & (ci <= ri)
    groups = C // GLA_DIAG
    tril = tril_ref[...]
    rsel = rsel_ref[...]
    gng = gng_ref[...]

    heads = range(GLA_HEADS)
    sk = [slice(h * GLA_DK, (h + 1) * GLA_DK) for h in heads]
    sv = [slice(h * GLA_DV, (h + 1) * GLA_DV) for h in heads]
    nt = (((1,), (1,)), ((), ()))

    def group_bcast(a, j):
        a3 = a.reshape(groups, GLA_DIAG, GLA_KW)
        return jnp.broadcast_to(a3[:, j:j + 1, :], (groups, GLA_DIAG, GLA_KW)).reshape(C, GLA_KW)

    def chunk(c, carry):
        rows = pl.ds(pl.multiple_of(c * C, C), C)
        q = q_ref[rows, :].astype(F32)
        k = k_ref[rows, :].astype(F32)
        hi, lo = _split_hi_lo(la_ref[rows, :])
        cs = jnp.dot(tril, jnp.concatenate([hi, lo], axis=1), preferred_element_type=F32)
        b = cs[:, :GLA_KW] + cs[:, GLA_KW:]

        a = [jnp.zeros((C, C), F32) for _ in heads]
        for li, s in enumerate(GLA_LEVELS):
            ref = jnp.concatenate(
                [jnp.broadcast_to(b[p + s - 1:p + s, :], (2 * s, GLA_KW)) for p in range(0, C, 2 * s)],
                axis=0)
            x = jnp.exp2(half_sign[li] * (b - ref))
            qx = (q * x).astype(_ACT)
            kx = (k * x).astype(_ACT)
            for h in heads:
                al = lax.dot_general(qx[:, sk[h]], kx[:, sk[h]], nt, preferred_element_type=F32)
                a[h] = jnp.where(pair_mask[li], al, a[h])
        ps = []
        for j in range(GLA_DIAG):
            dec = jnp.exp2(jnp.minimum(b - group_bcast(b, j), 0.0))
            ps.append((q * group_bcast(k, j) * dec).astype(_ACT))
        for h in heads:
            ad = jnp.dot(jnp.concatenate([p[:, sk[h]] for p in ps], axis=1), rsel,
                         preferred_element_type=F32)
            a[h] = jnp.where(diag_mask, ad, a[h])

        qe = (q * jnp.exp2(b)).astype(_ACT)
        b_last = b[C - 1:C, :]
        kd = (k * jnp.exp2(b_last - b)).astype(_ACT)
        keep = jnp.exp2(b_last)
        for h in heads:
            v = v_ref[rows, sv[h]]
            st = st_ref[h]
            o = lax.dot_general(qe[:, sk[h]], st.astype(_ACT), nt, preferred_element_type=F32)
            o = o + jnp.dot(a[h].astype(_ACT), v, preferred_element_type=F32)
            upd = lax.dot_general(v, kd[:, sk[h]], (((0,), (0,)), ((), ())), preferred_element_type=F32)
            st_ref[h] = st * keep[:, sk[h]] + upd

            ms = jnp.mean(o * o, axis=-1, keepdims=True)
            y = o * lax.rsqrt(ms + GLA_NORM_EPS) * gng
            o_ref[rows, sv[h]] = (y * _silu(g_ref[rows, sv[h]].astype(F32))).astype(_ACT)
        return carry

    lax.fori_loop(0, R // C, chunk, 0, unroll=2)


def _gla(gq, gk, gv, gg, la, gng, B, S, R):
    T = gq.shape[0]
    ns = S // R
    tril, rsel = _gla_constants()
    tril = jnp.asarray(tril, _ACT)
    rsel = jnp.asarray(rsel, _ACT)
    row = lambda b, i: (b * ns + i, 0)
    const = lambda b, i: (0, 0)
    rows = lambda w: pl.BlockSpec((R, w), row)
    whole = lambda a: pl.BlockSpec(a.shape, const)
    return pl.pallas_call(
        _gla_kernel,
        grid=(B, ns),
        in_specs=[rows(GLA_KW), rows(GLA_KW), rows(GLA_W), rows(GLA_W), rows(GLA_KW),
                  whole(tril), whole(rsel), whole(gng)],
        out_specs=rows(GLA_W),
        out_shape=jax.ShapeDtypeStruct((T, GLA_W), _ACT),
        scratch_shapes=[pltpu.VMEM((GLA_HEADS, GLA_DV, GLA_DK), F32)],
        compiler_params=pltpu.CompilerParams(
            dimension_semantics=("arbitrary", "arbitrary"), vmem_limit_bytes=_VMEM_LIMIT),
        name="gla",
    )(gq, gk, gv, gg, la, tril, rsel, gng)


def _diff_kernel(qT_ref, k_ref, vT_ref, g_ref, lq1_ref, lk1_ref, lq2_ref, lk2_ref, gcol_ref, o_ref,
                 s0_ref, s1_ref, bm0_ref, bm1_ref, m_ref, acc_ref, *, lam_init):
    s_refs, bm_refs = (s0_ref, s1_ref), (bm0_ref, bm1_ref)
    tq = qT_ref.shape[1]
    tk = s0_ref.shape[1]
    assert tq == 2 * tk
    qi = pl.program_id(2)
    nfull = 2 * qi
    qT = qT_ref[...]
    comp_row = lax.broadcasted_iota(jnp.int32, qT.shape, 0) < DIFF_DQK
    zero = jnp.zeros_like(qT)
    qcs = (jnp.where(comp_row, qT, zero), jnp.where(comp_row, zero, qT))
    acc_ref[...] = jnp.zeros_like(acc_ref)
    m_ref[...] = jnp.full_like(m_ref, -jnp.inf)

    def produce(ki, slot, c, w, lw=tk):
        ln = slice(w * lw, (w + 1) * lw)
        kb = k_ref[pl.ds(pl.multiple_of(ki * tk, tk), tk), :]
        s = jnp.dot(kb, qcs[c][:, ln], preferred_element_type=F32)
        s_refs[slot][c, :, ln] = s
        bm_refs[slot][c, :, ln] = jnp.max(s, axis=0, keepdims=True)

    def consume(ki, slot, c, w, masked=False, lw=tk):
        ln = slice(w * lw, (w + 1) * lw)
        vb = vT_ref[:, pl.ds(pl.multiple_of(ki * tk, tk), tk)]
        s = s_refs[slot][c, :, ln]
        if masked:
            key = lax.broadcasted_iota(jnp.int32, s.shape, 0)
            qry = lax.broadcasted_iota(jnp.int32, s.shape, 1)
            s = jnp.where(key <= qry, s, -jnp.inf)
            bm = jnp.max(s, axis=0, keepdims=True)
        else:
            bm = bm_refs[slot][c, :, ln]
        m_old = m_ref[c, :, ln]
        m_new = jnp.maximum(m_old, bm)
        p = jnp.exp2(s - m_new).astype(_ACT)
        acc_ref[c, :, ln] = (jnp.exp2(m_old - m_new) * acc_ref[c, :, ln]
                             + jnp.dot(vb, p, preferred_element_type=F32))
        m_ref[c, :, ln] = m_new

    lws = min(_DIFF_LANE_WINDOW, tk)

    def step(k_next, slot_next, k_cur, slot_cur):
        for c in range(2):
            for w in range(tq // lws):
                produce(k_next, slot_next, c, w, lws)
                consume(k_cur, slot_cur, c, w, lw=lws)

    for c in range(2):
        for w in range(2):
            produce(0, 0, c, w)

    def pair(j, carry):
        step(2 * j + 1, 1, 2 * j, 0)
        step(2 * j + 2, 0, 2 * j + 1, 1)
        return carry

    lax.fori_loop(0, qi, pair, 0)
    for c in range(2):
        produce(nfull + 1, 1, c, 1)
        consume(nfull, 0, c, 0, masked=True)
        consume(nfull, 0, c, 1)
    for c in range(2):
        consume(nfull + 1, 1, c, 1, masked=True)

    lam = (jnp.exp(jnp.sum(lq1_ref[...] * lk1_ref[...], axis=-1, keepdims=True))
           - jnp.exp(jnp.sum(lq2_ref[...] * lk2_ref[...], axis=-1, keepdims=True)) + lam_init)
    l1 = acc_ref[0, DIFF_DV:DIFF_DV + 1, :]
    l2 = acc_ref[1, DIFF_DV:DIFF_DV + 1, :]
    oT = acc_ref[0, 0:DIFF_DV, :] / l1 - lam * (acc_ref[1, 0:DIFF_DV, :] / l2)
    ms = jnp.mean(oT * oT, axis=0, keepdims=True)
    yT = oT * lax.rsqrt(ms + DIFF_NORM_EPS) * (gcol_ref[...] * (1.0 - lam_init))
    o_ref[...] = (yT.T * _silu(g_ref[...].astype(F32))).astype(_ACT)


def _diff(dqT, dk, dvT, dg, lq1, lk1, lq2, lk2, gcol, B, S, tq, lam_init):
    T = dk.shape[0]
    nq = S // tq
    small = lambda a: pl.BlockSpec(a.shape, lambda b, h, i: (0, 0))
    blk = pl.BlockSpec((tq, DIFF_DV), lambda b, h, i: (b * nq + i, h))
    return pl.pallas_call(
        functools.partial(_diff_kernel, lam_init=lam_init),
        grid=(B, DIFF_HEADS, nq),
        in_specs=[pl.BlockSpec((None, DIFF_DV, tq), lambda b, h, i: (b, h, i)),
                  pl.BlockSpec((S, DIFF_DV), lambda b, h, i: (b, h)),
                  pl.BlockSpec((None, DIFF_VT_ROWS, S), lambda b, h, i: (b, h, 0)),
                  blk, small(lq1), small(lk1), small(lq2), small(lk2), small(gcol)],
        out_specs=blk,
        out_shape=jax.ShapeDtypeStruct((T, DIFF_W), _ACT),
        scratch_shapes=[pltpu.VMEM((2, tq // 2, tq), F32), pltpu.VMEM((2, tq // 2, tq), F32),
                        pltpu.VMEM((2, 1, tq), F32), pltpu.VMEM((2, 1, tq), F32),
                        pltpu.VMEM((2, 1, tq), F32), pltpu.VMEM((2, DIFF_VT_ROWS, tq), F32)],
        compiler_params=pltpu.CompilerParams(
            dimension_semantics=("arbitrary", "arbitrary", "arbitrary"), vmem_limit_bytes=_VMEM_LIMIT),
        name="diffattn",
    )(dqT, dk, dvT, dg, lq1, lk1, lq2, lk2, gcol)


def _outproj_kernel(gla_ref, dif_ref, xat_ref, x_ref, w_ref, g_ref, b_ref, o_ref, *, alpha):
    y = alpha * x_ref[...]
    y = y + jnp.dot(gla_ref[...], w_ref[0:GLA_W, :], preferred_element_type=F32)
    y = y + jnp.dot(dif_ref[...], w_ref[GLA_W:GLA_W + DIFF_W, :], preferred_element_type=F32)
    y = y + jnp.dot(xat_ref[...], w_ref[GLA_W + DIFF_W:GLA_W + DIFF_W + MEM_W, :], preferred_element_type=F32)
    mu = jnp.mean(y, axis=-1, keepdims=True)
    yc = y - mu
    var = jnp.mean(yc * yc, axis=-1, keepdims=True)
    o_ref[...] = yc * lax.rsqrt(var + LN_EPS) * g_ref[...] + b_ref[...]


def _outproj(gla, dif, xat, x2d, w_out, ln_g, ln_b, tm, alpha):
    T, D = x2d.shape
    row = lambda i: (i, 0)
    const = lambda i: (0, 0)
    rows = lambda w: pl.BlockSpec((tm, w), row)
    return pl.pallas_call(
        functools.partial(_outproj_kernel, alpha=alpha),
        grid=(T // tm,),
        in_specs=[rows(GLA_W), rows(DIFF_W), rows(MEM_W), rows(D),
                  pl.BlockSpec(w_out.shape, const, pipeline_mode=pl.Buffered(1)),
                  pl.BlockSpec(ln_g.shape, const), pl.BlockSpec(ln_b.shape, const)],
        out_specs=rows(D),
        out_shape=jax.ShapeDtypeStruct((T, D), F32),
        compiler_params=pltpu.CompilerParams(
            dimension_semantics=("arbitrary",), vmem_limit_bytes=_VMEM_LIMIT),
        name="outproj",
    )(gla, dif, xat, x2d, w_out, ln_g, ln_b)


def _rotary_tables(positions):
    B, S = positions.shape
    half = ROT_DIM // 2
    inv_freq = jnp.power(jnp.float32(ROPE_THETA), -(jnp.arange(0, ROT_DIM, 2, dtype=F32) / ROT_DIM))
    lane = np.arange(LANES) % DIFF_DQK
    lane_freq = jnp.where(lane < ROT_DIM, inv_freq[lane % half], 0.0)
    ang = positions.astype(F32).reshape(B * S, 1) * lane_freq[None, :]
    return jnp.cos(ang), jnp.sin(ang)


def _tile(default, n):
    t = min(default, n)
    assert n % t == 0, (n, t)
    return t


def kernel(x, mem, positions, w_in, w_gk_up, b_gk_up, gla_norm_g, lambda_q1, lambda_k1, lambda_q2,
           lambda_k2, diff_norm_g, w_mem_kv, w_out, ln_g, ln_b):
    B, S, D = x.shape
    M = mem.shape[1]
    depth = w_in.shape[0]
    T = B * S
    assert S % GLA_CHUNK == 0 and M % LANES == 0
    tm_in, r_gla, tq, tm_out = _tile(_TM_IN, S), _tile(_R_GLA, S), _tile(_TQ, S), _tile(_TM_OUT, S)
    alpha = (2.0 * depth) ** 0.25

    cos_t, sin_t = _rotary_tables(positions)
    mem2d = mem.reshape(B * M, D)
    h = x.reshape(T, D)

    o_lr = 2 * GLA_KW + 2 * GLA_W
    o_d = o_lr + GLA_LOWRANK
    o_m = o_d + 4 * DIFF_W
    for l in range(depth):
        w = w_in[l]
        w_all = jnp.concatenate(
            [w[:, :o_lr], w[:, o_d:], w[:, o_lr:o_d], jnp.zeros((D, LANES - GLA_LOWRANK), w.dtype)],
            axis=1).astype(_ACT)
        wgk = jnp.pad(w_gk_up[l].astype(F32), ((0, LANES - GLA_LOWRANK), (0, 0)))
        wgk_hi, wgk_lo = _split_hi_lo(wgk)
        wgk3 = jnp.concatenate([wgk_hi, wgk_lo, wgk_hi], axis=0)
        bgk = b_gk_up[l].astype(F32).reshape(1, GLA_KW)

        mkT, mv = _memkv(mem2d, w_mem_kv[l].astype(_ACT), B, M)
        gq, gk, gv, gg, la, dqT, dk, dvT, dg, xat = _inproj(
            h, w_all, wgk3, bgk, cos_t, sin_t, mkT, mv, B, S, M, tm_in)
        gla = _gla(gq, gk, gv, gg, la, gla_norm_g[l].astype(F32).reshape(1, GLA_DV), B, S, r_gla)
        row64 = lambda a: a[l].astype(F32).reshape(1, DIFF_DQK)
        dif = _diff(dqT, dk, dvT, dg, row64(lambda_q1), row64(lambda_k1), row64(lambda_q2),
                    row64(lambda_k2), diff_norm_g[l].astype(F32).reshape(DIFF_DV, 1), B, S, tq,
                    _lambda_init(l))
        h = _outproj(gla, dif, xat, h, w_out[l].astype(_ACT), ln_g[l].astype(F32).reshape(1, D),
                     ln_b[l].astype(F32).reshape(1, D), tm_out, alpha)
    return h.reshape(B, S, D)
```

```python
import functools
import math

import jax
import jax.numpy as jnp
import numpy as np
from jax import lax
from jax.experimental import pallas as pl
from jax.experimental.pallas import tpu as pltpu

F32 = jnp.float32
_ACT = jnp.bfloat16

GLA_HEADS = 4
GLA_DK = 128
GLA_DV = 256
GLA_LOWRANK = 16
GLA_TAU = 16.0
GLA_NORM_EPS = 1e-6
GLA_W = GLA_HEADS * GLA_DV
GLA_KW = GLA_HEADS * GLA_DK

DIFF_HEADS = 4
DIFF_DV = 128
DIFF_DQK = 64
DIFF_NORM_EPS = 1e-5
DIFF_W = DIFF_HEADS * DIFF_DV
ACT_SUBLANES = 16
DIFF_VT_ROWS = DIFF_DV + ACT_SUBLANES
LOG2E = math.log2(math.e)

MEM_HEADS = 4
MEM_DH = 128
MEM_W = MEM_HEADS * MEM_DH

ROPE_THETA = 500000.0
ROT_DIM = DIFF_DQK // 4
LN_EPS = 1e-5

LANES = 128
SUBLANES = 8

W_OFF_GLA = 0
W_OFF_DIFF = W_OFF_GLA + 2 * GLA_KW + 2 * GLA_W
W_OFF_MEM = W_OFF_DIFF + 4 * DIFF_W
W_OFF_LR = W_OFF_MEM + 2 * MEM_W
W_COLS = W_OFF_LR + LANES

GLA_CHUNK = 64
GLA_LEVELS = (32, 16, 8)
GLA_DIAG = 8

_TM_IN = 256
_R_GLA = 512
_TQ = 1024
_TM_OUT = 512
_DIFF_LANE_WINDOW = 256
_VMEM_LIMIT = 60 * 1024 * 1024


def _lambda_init(layer):
    return 0.8 - 0.6 * math.exp(-0.3 * layer)


def _silu(g):
    half = 0.5 * g
    return half + half * jnp.tanh(half)


def _split_hi_lo(a):
    hi = a.astype(_ACT)
    lo = (a - hi.astype(F32)).astype(_ACT)
    return hi, lo


def _memkv_kernel(mem_ref, w_ref, mkT_ref, mv_ref):
    kv = jnp.dot(mem_ref[...].astype(_ACT), w_ref[...], preferred_element_type=F32)
    mkT_ref[...] = kv[:, :MEM_W].T.astype(_ACT)
    mv_ref[...] = kv[:, MEM_W:].astype(_ACT)


def _memkv(mem2d, w_kv, B, M):
    D = mem2d.shape[1]
    return pl.pallas_call(
        _memkv_kernel,
        grid=(B,),
        in_specs=[pl.BlockSpec((M, D), lambda b: (b, 0)),
                  pl.BlockSpec((D, 2 * MEM_W), lambda b: (0, 0))],
        out_specs=[pl.BlockSpec((None, MEM_W, M), lambda b: (b, 0, 0)),
                   pl.BlockSpec((M, MEM_W), lambda b: (b, 0))],
        out_shape=[jax.ShapeDtypeStruct((B, MEM_W, M), _ACT),
                   jax.ShapeDtypeStruct((B * M, MEM_W), _ACT)],
        compiler_params=pltpu.CompilerParams(vmem_limit_bytes=_VMEM_LIMIT),
        name="memkv",
    )(mem2d, w_kv)


def _inproj_kernel(x_ref, w_ref, wgk_ref, bgk_ref, cos_ref, sin_ref, mkT_ref, mv_ref,
                   gq_ref, gk_ref, gv_ref, gg_ref, la_ref, dqT_ref, dk_ref, dvT_ref, dg_ref, xat_ref):
    tm = x_ref.shape[0]
    xb = x_ref[...].astype(_ACT)

    def proj(base, c0, c1):
        return jnp.dot(xb, w_ref[:, base + c0:base + c1], preferred_element_type=F32)

    gq_ref[...] = (proj(W_OFF_GLA,0, GLA_KW) * (GLA_DK ** -0.5)).astype(_ACT)
    gk_ref[...] = proj(W_OFF_GLA,GLA_KW, 2 * GLA_KW).astype(_ACT)
    gv_ref[...] = proj(W_OFF_GLA,2 * GLA_KW, 2 * GLA_KW + GLA_W).astype(_ACT)
    gg_ref[...] = proj(W_OFF_GLA,2 * GLA_KW + GLA_W, 2 * GLA_KW + 2 * GLA_W).astype(_ACT)

    glr = proj(W_OFF_LR,0, LANES)
    hi, lo = _split_hi_lo(glr)
    logit = jnp.dot(jnp.concatenate([hi, hi, lo], axis=1), wgk_ref[...],
                    preferred_element_type=F32) + bgk_ref[...]
    log_sig = jnp.minimum(logit, 0.0) - jnp.log(1.0 + jnp.exp(-jnp.abs(logit)))
    la_ref[...] = log_sig * (LOG2E / GLA_TAU)

    lane = lax.broadcasted_iota(jnp.int32, (tm, LANES), 1) & (DIFF_DQK - 1)
    cosv = cos_ref[...]
    sinv = sin_ref[...]
    half = ROT_DIM // 2

    def rope(a):
        up = pltpu.roll(a, LANES - half, 1)
        dn = pltpu.roll(a, half, 1)
        return a * cosv + jnp.where(lane < half, -up, dn) * sinv

    dq = proj(W_OFF_DIFF,0, DIFF_W)
    dk = proj(W_OFF_DIFF,DIFF_W, 2 * DIFF_W)
    dv = proj(W_OFF_DIFF,2 * DIFF_W, 3 * DIFF_W)
    ones = jnp.ones((ACT_SUBLANES, tm), _ACT)
    for h in range(DIFF_HEADS):
        sl = slice(h * DIFF_DV, (h + 1) * DIFF_DV)
        dqT_ref[sl, :] = (rope(dq[:, sl]) * (DIFF_DQK ** -0.5 * LOG2E)).T.astype(_ACT)
        dk_ref[:, sl] = rope(dk[:, sl]).astype(_ACT)
        dvT_ref[h * DIFF_VT_ROWS:h * DIFF_VT_ROWS + DIFF_DV, :] = dv[:, sl].T.astype(_ACT)
        dvT_ref[h * DIFF_VT_ROWS + DIFF_DV:(h + 1) * DIFF_VT_ROWS, :] = ones
    dg_ref[...] = proj(W_OFF_DIFF,3 * DIFF_W, 4 * DIFF_W).astype(_ACT)

    mq = proj(W_OFF_MEM,0, MEM_W)
    mg = proj(W_OFF_MEM,MEM_W, 2 * MEM_W)
    for h in range(MEM_HEADS):
        sl = slice(h * MEM_DH, (h + 1) * MEM_DH)
        qh = (mq[:, sl] * (MEM_DH ** -0.5)).astype(_ACT)
        s = jnp.dot(qh, mkT_ref[sl, :], preferred_element_type=F32)
        p = jnp.exp(s - jnp.max(s, axis=-1, keepdims=True))
        l = jnp.sum(p, axis=-1, keepdims=True)
        o = jnp.dot(p.astype(_ACT), mv_ref[:, sl], preferred_element_type=F32) / l
        xat_ref[:, sl] = (o * _silu(mg[:, sl])).astype(_ACT)


def _inproj(x2d, w_all, wgk3, bgk, cos_t, sin_t, mkT, mv, B, S, M, tm):
    T, D = x2d.shape
    ns = S // tm
    row = lambda b, i: (b * ns + i, 0)
    const = lambda b, i: (0, 0)
    whole = lambda a: pl.BlockSpec(a.shape, const, pipeline_mode=pl.Buffered(1))
    rows = lambda w: pl.BlockSpec((tm, w), row)
    tr = pl.BlockSpec((None, DIFF_W, tm), lambda b, i: (b, 0, i))
    trv = pl.BlockSpec((None, DIFF_HEADS * DIFF_VT_ROWS, tm), lambda b, i: (b, 0, i))
    return pl.pallas_call(
        _inproj_kernel,
        grid=(B, ns),
        in_specs=[rows(D), whole(w_all), whole(wgk3), whole(bgk),
                  rows(LANES), rows(LANES),
                  pl.BlockSpec((None, MEM_W, M), lambda b, i: (b, 0, 0)),
                  pl.BlockSpec((M, MEM_W), lambda b, i: (b, 0))],
        out_specs=[rows(GLA_KW), rows(GLA_KW), rows(GLA_W), rows(GLA_W), rows(GLA_KW),
                   tr, rows(DIFF_W), trv, rows(DIFF_W), rows(MEM_W)],
        out_shape=[jax.ShapeDtypeStruct((T, GLA_KW), _ACT), jax.ShapeDtypeStruct((T, GLA_KW), _ACT),
                   jax.ShapeDtypeStruct((T, GLA_W), _ACT), jax.ShapeDtypeStruct((T, GLA_W), _ACT),
                   jax.ShapeDtypeStruct((T, GLA_KW), F32),
                   jax.ShapeDtypeStruct((B, DIFF_W, S), _ACT), jax.ShapeDtypeStruct((T, DIFF_W), _ACT),
                   jax.ShapeDtypeStruct((B, DIFF_HEADS * DIFF_VT_ROWS, S), _ACT),
                   jax.ShapeDtypeStruct((T, DIFF_W), _ACT),
                   jax.ShapeDtypeStruct((T, MEM_W), _ACT)],
        compiler_params=pltpu.CompilerParams(
            dimension_semantics=("arbitrary", "arbitrary"), vmem_limit_bytes=_VMEM_LIMIT),
        name="inproj",
    )(x2d, w_all, wgk3, bgk, cos_t, sin_t, mkT, mv)


def _gla_constants():
    C = GLA_CHUNK
    t = np.arange(C)
    tril = (t[None, :] <= t[:, None]).astype(np.float32)
    rsel = np.zeros((GLA_DIAG * GLA_DK, C), np.float32)
    for j in range(GLA_DIAG):
        rsel[j * GLA_DK:(j + 1) * GLA_DK, j::GLA_DIAG] = 1.0
    return tril, rsel


def _gla_kernel(q_ref, k_ref, v_ref, g_ref, la_ref, tril_ref, rsel_ref, gng_ref, o_ref, st_ref):
    R = q_ref.shape[0]
    C = GLA_CHUNK

    @pl.when(pl.program_id(1) == 0)
    def _():
        st_ref[...] = jnp.zeros_like(st_ref)

    ri = lax.broadcasted_iota(jnp.int32, (C, C), 0)
    ci = lax.broadcasted_iota(jnp.int32, (C, C), 1)
    rk = lax.broadcasted_iota(jnp.int32, (C, GLA_KW), 0)
    pair_mask = [((ri // (2 * s)) == (ci // (2 * s))) & (((ri // s) & 1) == 1) & (((ci // s) & 1) == 0)
                 for s in GLA_LEVELS]
    half_sign = [jnp.where(((rk // s) & 1) == 1, 1.0, -1.0) for s in GLA_LEVELS]
    diag_mask = ((ri // GLA_DIAG) == (ci // GLA_DIAG)) & (ci <= ri)
    groups = C // GLA_DIAG
    tril = tril_ref[...]
    rsel = rsel_ref[...]
    gng = gng_ref[...]

    heads = range(GLA_HEADS)
    sk = [slice(h * GLA_DK, (h + 1) * GLA_DK) for h in heads]
    sv = [slice(h * GLA_DV, (h + 1) * GLA_DV) for h in heads]
    nt = (((1,), (1,)), ((), ()))

    def group_bcast(a, j):
        a3 = a.reshape(groups, GLA_DIAG, GLA_KW)
        return jnp.broadcast_to(a3[:, j:j + 1, :], (groups, GLA_DIAG, GLA_KW)).reshape(C, GLA_KW)

    def chunk(c, carry):
        rows = pl.ds(pl.multiple_of(c * C, C), C)
        q = q_ref[rows, :].astype(F32)
        k = k_ref[rows, :].astype(F32)
        hi, lo = _split_hi_lo(la_ref[rows, :])
        cs = jnp.dot(tril, jnp.concatenate([hi, lo], axis=1), preferred_element_type=F32)
        b = cs[:, :GLA_KW] + cs[:, GLA_KW:]

        a = [jnp.zeros((C, C), F32) for _ in heads]
        for li, s in enumerate(GLA_LEVELS):
            ref = jnp.concatenate(
                [jnp.broadcast_to(b[p + s - 1:p + s, :], (2 * s, GLA_KW)) for p in range(0, C, 2 * s)],
                axis=0)
            x = jnp.exp2(half_sign[li] * (b - ref))
            qx = (q * x).astype(_ACT)
            kx = (k * x).astype(_ACT)
            for h in heads:
                al = lax.dot_general(qx[:, sk[h]], kx[:, sk[h]], nt, preferred_element_type=F32)
                a[h] = jnp.where(pair_mask[li], al, a[h])
        ps = []
        for j in range(GLA_DIAG):
            dec = jnp.exp2(jnp.minimum(b - group_bcast(b, j), 0.0))
            ps.append((q * group_bcast(k, j) * dec).astype(_ACT))
        for h in heads:
            ad = jnp.dot(jnp.concatenate([p[:, sk[h]] for p in ps], axis=1), rsel,
                         preferred_element_type=F32)
            a[h] = jnp.where(diag_mask, ad, a[h])

        qe = (q * jnp.exp2(b)).astype(_ACT)
        b_last = b[C - 1:C, :]
        kd = (k * jnp.exp2(b_last - b)).astype(_ACT)
        keep = jnp.exp2(b_last)
        for h in heads:
            v = v_ref[rows, sv[h]]
            st = st_ref[h]
            o = lax.dot_general(qe[:, sk[h]], st.astype(_ACT), nt, preferred_element_type=F32)
            o = o + jnp.dot(a[h].astype(_ACT), v, preferred_element_type=F32)
            upd = lax.dot_general(v, kd[:, sk[h]], (((0,), (0,)), ((), ())), preferred_element_type=F32)
            st_ref[h] = st * keep[:, sk[h]] + upd

            ms = jnp.mean(o * o, axis=-1, keepdims=True)
            y = o * lax.rsqrt(ms + GLA_NORM_EPS) * gng
            o_ref[rows, sv[h]] = (y * _silu(g_ref[rows, sv[h]].astype(F32))).astype(_ACT)
        return carry

    lax.fori_loop(0, R // C, chunk, 0, unroll=8)


def _gla(gq, gk, gv, gg, la, gng, B, S, R):
    T = gq.shape[0]
    ns = S // R
    tril, rsel = _gla_constants()
    tril = jnp.asarray(tril, _ACT)
    rsel = jnp.asarray(rsel, _ACT)
    row = lambda b, i: (b * ns + i, 0)
    const = lambda b, i: (0, 0)
    rows = lambda w: pl.BlockSpec((R, w), row)
    whole = lambda a: pl.BlockSpec(a.shape, const)
    return pl.pallas_call(
        _gla_kernel,
        grid=(B, ns),
        in_specs=[rows(GLA_KW), rows(GLA_KW), rows(GLA_W), rows(GLA_W), rows(GLA_KW),
                  whole(tril), whole(rsel), whole(gng)],
        out_specs=rows(GLA_W),
        out_shape=jax.ShapeDtypeStruct((T, GLA_W), _ACT),
        scratch_shapes=[pltpu.VMEM((GLA_HEADS, GLA_DV, GLA_DK), F32)],
        compiler_params=pltpu.CompilerParams(
            dimension_semantics=("arbitrary", "arbitrary"), vmem_limit_bytes=_VMEM_LIMIT),
        name="gla",
    )(gq, gk, gv, gg, la, tril, rsel, gng)


def _diff_kernel(qT_ref, qTn_ref, k_ref, vT_ref, g_ref, lq1_ref, lk1_ref, lq2_ref, lk2_ref, gcol_ref, o_ref,
                 s0_ref, s1_ref, bm0_ref, bm1_ref, m_ref, acc_ref, *, lam_init):
    s_refs, bm_refs = (s0_ref, s1_ref), (bm0_ref, bm1_ref)
    tq = qT_ref.shape[1]
    tk = s0_ref.shape[1]
    assert tq == 2 * tk
    qi = pl.program_id(2)
    nfull = 2 * qi
    comp_row = lax.broadcasted_iota(jnp.int32, qT_ref.shape, 0) < DIFF_DQK

    def components(q):
        zero = jnp.zeros_like(q)
        return jnp.where(comp_row, q, zero), jnp.where(comp_row, zero, q)

    qcs = components(qT_ref[...])
    acc_ref[...] = jnp.zeros_like(acc_ref)
    m_ref[...] = jnp.full_like(m_ref, -jnp.inf)

    def produce(ki, slot, c, w, lw=tk, q=None):
        ln = slice(w * lw, (w + 1) * lw)
        kb = k_ref[pl.ds(pl.multiple_of(ki * tk, tk), tk), :]
        s = jnp.dot(kb, (qcs if q is None else q)[c][:, ln], preferred_element_type=F32)
        s_refs[slot][c, :, ln] = s
        bm_refs[slot][c, :, ln] = jnp.max(s, axis=0, keepdims=True)

    def consume(ki, slot, c, w, masked=False, lw=tk):
        ln = slice(w * lw, (w + 1) * lw)
        vb = vT_ref[:, pl.ds(pl.multiple_of(ki * tk, tk), tk)]
        s = s_refs[slot][c, :, ln]
        if masked:
            key = lax.broadcasted_iota(jnp.int32, s.shape, 0)
            qry = lax.broadcasted_iota(jnp.int32, s.shape, 1)
            s = jnp.where(key <= qry, s, -jnp.inf)
            bm = jnp.max(s, axis=0, keepdims=True)
        else:
            bm = bm_refs[slot][c, :, ln]
        m_old = m_ref[c, :, ln]
        m_new = jnp.maximum(m_old, bm)
        p = jnp.exp2(s - m_new).astype(_ACT)
        acc_ref[c, :, ln] = (jnp.exp2(m_old - m_new) * acc_ref[c, :, ln]
                             + jnp.dot(vb, p, preferred_element_type=F32))
        m_ref[c, :, ln] = m_new

    lws = min(_DIFF_LANE_WINDOW, tk)

    def step(k_next, slot_next, k_cur, slot_cur):
        for c in range(2):
            for w in range(tq // lws):
                produce(k_next, slot_next, c, w, lws)
                consume(k_cur, slot_cur, c, w, lw=lws)

    @pl.when(qi == 0)
    def _():
        for c in range(2):
            for w in range(2):
                produce(0, 0, c, w)

    def pair(j, carry):
        step(2 * j + 1, 1, 2 * j, 0)
        step(2 * j + 2, 0, 2 * j + 1, 1)
        return carry

    lax.fori_loop(0, qi, pair, 0)
    qn = components(qTn_ref[...])
    produce(nfull + 1, 1, 0, 1)
    consume(nfull, 0, 0, 0, masked=True)
    consume(nfull, 0, 0, 1)
    produce(nfull + 1, 1, 1, 1)
    consume(nfull, 0, 1, 0, masked=True)
    produce(0, 0, 0, 0, q=qn)
    consume(nfull, 0, 1, 1)
    produce(0, 0, 0, 1, q=qn)
    consume(nfull + 1, 1, 0, 1, masked=True)
    produce(0, 0, 1, 0, q=qn)
    consume(nfull + 1, 1, 1, 1, masked=True)
    produce(0, 0, 1, 1, q=qn)

    lam = (jnp.exp(jnp.sum(lq1_ref[...] * lk1_ref[...], axis=-1, keepdims=True))
           - jnp.exp(jnp.sum(lq2_ref[...] * lk2_ref[...], axis=-1, keepdims=True)) + lam_init)
    l1 = acc_ref[0, DIFF_DV:DIFF_DV + 1, :]
    l2 = acc_ref[1, DIFF_DV:DIFF_DV + 1, :]
    oT = acc_ref[0, 0:DIFF_DV, :] / l1 - lam * (acc_ref[1, 0:DIFF_DV, :] / l2)
    ms = jnp.mean(oT * oT, axis=0, keepdims=True)
    yT = oT * lax.rsqrt(ms + DIFF_NORM_EPS) * (gcol_ref[...] * (1.0 - lam_init))
    o_ref[...] = (yT.T * _silu(g_ref[...].astype(F32))).astype(_ACT)


def _diff(dqT, dk, dvT, dg, lq1, lk1, lq2, lk2, gcol, B, S, tq, lam_init):
    T = dk.shape[0]
    nq = S // tq
    small = lambda a: pl.BlockSpec(a.shape, lambda b, h, i: (0, 0))
    blk = pl.BlockSpec((tq, DIFF_DV), lambda b, h, i: (b * nq + i, h))
    return pl.pallas_call(
        functools.partial(_diff_kernel, lam_init=lam_init),
        grid=(B, DIFF_HEADS, nq),
        in_specs=[pl.BlockSpec((None, DIFF_DV, tq), lambda b, h, i: (b, h, i)),
                  pl.BlockSpec((None, DIFF_DV, tq), lambda b, h, i: (b, h, jnp.minimum(i + 1, nq - 1))),
                  pl.BlockSpec((S, DIFF_DV), lambda b, h, i: (b, h)),
                  pl.BlockSpec((None, DIFF_VT_ROWS, S), lambda b, h, i: (b, h, 0)),
                  blk, small(lq1), small(lk1), small(lq2), small(lk2), small(gcol)],
        out_specs=blk,
        out_shape=jax.ShapeDtypeStruct((T, DIFF_W), _ACT),
        scratch_shapes=[pltpu.VMEM((2, tq // 2, tq), F32), pltpu.VMEM((2, tq // 2, tq), F32),
                        pltpu.VMEM((2, 1, tq), F32), pltpu.VMEM((2, 1, tq), F32),
                        pltpu.VMEM((2, 1, tq), F32), pltpu.VMEM((2, DIFF_VT_ROWS, tq), F32)],
        compiler_params=pltpu.CompilerParams(
            dimension_semantics=("arbitrary", "arbitrary", "arbitrary"), vmem_limit_bytes=_VMEM_LIMIT),
        name="diffattn",
    )(dqT, dqT, dk, dvT, dg, lq1, lk1, lq2, lk2, gcol)


def _outproj_kernel(gla_ref, dif_ref, xat_ref, x_ref, w_ref, g_ref, b_ref, o_ref, *, alpha):
    y = alpha * x_ref[...]
    y = y + jnp.dot(gla_ref[...], w_ref[0:GLA_W, :], preferred_element_type=F32)
    y = y + jnp.dot(dif_ref[...], w_ref[GLA_W:GLA_W + DIFF_W, :], preferred_element_type=F32)
    y = y + jnp.dot(xat_ref[...], w_ref[GLA_W + DIFF_W:GLA_W + DIFF_W + MEM_W, :], preferred_element_type=F32)
    mu = jnp.mean(y, axis=-1, keepdims=True)
    yc = y - mu
    var = jnp.mean(yc * yc, axis=-1, keepdims=True)
    o_ref[...] = yc * lax.rsqrt(var + LN_EPS) * g_ref[...] + b_ref[...]


def _outproj(gla, dif, xat, x2d, w_out, ln_g, ln_b, tm, alpha):
    T, D = x2d.shape
    row = lambda i: (i, 0)
    const = lambda i: (0, 0)
    rows = lambda w: pl.BlockSpec((tm, w), row)
    return pl.pallas_call(
        functools.partial(_outproj_kernel, alpha=alpha),
        grid=(T // tm,),
        in_specs=[rows(GLA_W), rows(DIFF_W), rows(MEM_W), rows(D),
                  pl.BlockSpec(w_out.shape, const, pipeline_mode=pl.Buffered(1)),
                  pl.BlockSpec(ln_g.shape, const), pl.BlockSpec(ln_b.shape, const)],
        out_specs=rows(D),
        out_shape=jax.ShapeDtypeStruct((T, D), F32),
        compiler_params=pltpu.CompilerParams(
            dimension_semantics=("arbitrary",), vmem_limit_bytes=_VMEM_LIMIT),
        name="outproj",
    )(gla, dif, xat, x2d, w_out, ln_g, ln_b)


def _rotary_tables(positions):
    B, S = positions.shape
    half = ROT_DIM // 2
    inv_freq = jnp.power(jnp.float32(ROPE_THETA), -(jnp.arange(0, ROT_DIM, 2, dtype=F32) / ROT_DIM))
    lane = np.arange(LANES) % DIFF_DQK
    lane_freq = jnp.where(lane < ROT_DIM, inv_freq[lane % half], 0.0)
    ang = positions.astype(F32).reshape(B * S, 1) * lane_freq[None, :]
    return jnp.cos(ang), jnp.sin(ang)


def _tile(default, n):
    t = min(default, n)
    assert n % t == 0, (n, t)
    return t


def kernel(x, mem, positions, w_in, w_gk_up, b_gk_up, gla_norm_g, lambda_q1, lambda_k1, lambda_q2,
           lambda_k2, diff_norm_g, w_mem_kv, w_out, ln_g, ln_b):
    B, S, D = x.shape
    M = mem.shape[1]
    depth = w_in.shape[0]
    T = B * S
    assert S % GLA_CHUNK == 0 and M % LANES == 0
    tm_in, r_gla, tq, tm_out = _tile(_TM_IN, S), _tile(_R_GLA, S), _tile(_TQ, S), _tile(_TM_OUT, S)
    alpha = (2.0 * depth) ** 0.25

    cos_t, sin_t = _rotary_tables(positions)
    mem2d = mem.reshape(B * M, D)
    h = x.reshape(T, D)

    o_lr = 2 * GLA_KW + 2 * GLA_W
    o_d = o_lr + GLA_LOWRANK
    o_m = o_d + 4 * DIFF_W
    for l in range(depth):
        w = w_in[l]
        w_all = jnp.concatenate(
            [w[:, :o_lr], w[:, o_d:], w[:, o_lr:o_d], jnp.zeros((D, LANES - GLA_LOWRANK), w.dtype)],
            axis=1).astype(_ACT)
        wgk = jnp.pad(w_gk_up[l].astype(F32), ((0, LANES - GLA_LOWRANK), (0, 0)))
        wgk_hi, wgk_lo = _split_hi_lo(wgk)
        wgk3 = jnp.concatenate([wgk_hi, wgk_lo, wgk_hi], axis=0)
        bgk = b_gk_up[l].astype(F32).reshape(1, GLA_KW)

        mkT, mv = _memkv(mem2d, w_mem_kv[l].astype(_ACT), B, M)
        gq, gk, gv, gg, la, dqT, dk, dvT, dg, xat = _inproj(
            h, w_all, wgk3, bgk, cos_t, sin_t, mkT, mv, B, S, M, tm_in)
        gla = _gla(gq, gk, gv, gg, la, gla_norm_g[l].astype(F32).reshape(1, GLA_DV), B, S, r_gla)
        row64 = lambda a: a[l].astype(F32).reshape(1, DIFF_DQK)
        dif = _diff(dqT, dk, dvT, dg, row64(lambda_q1), row64(lambda_k1), row64(lambda_q2),
                    row64(lambda_k2), diff_norm_g[l].astype(F32).reshape(DIFF_DV, 1), B, S, tq,
                    _lambda_init(l))
        h = _outproj(gla, dif, xat, h, w_out[l].astype(_ACT), ln_g[l].astype(F32).reshape(1, D),
                     ln_b[l].astype(F32).reshape(1, D), tm_out, alpha)
    return h.reshape(B, S, D)
```

```python
import functools
import math

import jax
import jax.numpy as jnp
import numpy as np
from jax import lax
from jax.experimental import pallas as pl
from jax.experimental.pallas import tpu as pltpu

F32 = jnp.float32
_ACT = jnp.bfloat16

GLA_HEADS = 4
GLA_DK = 128
GLA_DV = 256
GLA_LOWRANK = 16
GLA_TAU = 16.0
GLA_NORM_EPS = 1e-6
GLA_W = GLA_HEADS * GLA_DV
GLA_KW = GLA_HEADS * GLA_DK

DIFF_HEADS = 4
DIFF_DV = 128
DIFF_DQK = 64
DIFF_NORM_EPS = 1e-5
DIFF_W = DIFF_HEADS * DIFF_DV
ACT_SUBLANES = 16
DIFF_VT_ROWS = DIFF_DV + ACT_SUBLANES
LOG2E = math.log2(math.e)

MEM_HEADS = 4
MEM_DH = 128
MEM_W = MEM_HEADS * MEM_DH

ROPE_THETA = 500000.0
ROT_DIM = DIFF_DQK // 4
LN_EPS = 1e-5

LANES = 128
SUBLANES = 8

W_OFF_GLA = 0
W_OFF_DIFF = W_OFF_GLA + 2 * GLA_KW + 2 * GLA_W
W_OFF_MEM = W_OFF_DIFF + 4 * DIFF_W
W_OFF_LR = W_OFF_MEM + 2 * MEM_W
W_COLS = W_OFF_LR + LANES

GLA_CHUNK = 64
GLA_LEVELS = (32, 16, 8)
GLA_DIAG = 8

_TM_IN = 256
_R_GLA = 512
_TQ = 1024
_TM_OUT = 512
_DIFF_LANE_WINDOW = 256
_DIFF_PAIRS_PER_TRIP = 4
_VMEM_LIMIT = 60 * 1024 * 1024


def _lambda_init(layer):
    return 0.8 - 0.6 * math.exp(-0.3 * layer)


def _silu(g):
    half = 0.5 * g
    return half + half * jnp.tanh(half)


def _split_hi_lo(a):
    hi = a.astype(_ACT)
    lo = (a - hi.astype(F32)).astype(_ACT)
    return hi, lo


def _memkv_kernel(mem_ref, w_ref, mkT_ref, mv_ref):
    kv = jnp.dot(mem_ref[...].astype(_ACT), w_ref[...], preferred_element_type=F32)
    mkT_ref[...] = kv[:, :MEM_W].T.astype(_ACT)
    mv_ref[...] = kv[:, MEM_W:].astype(_ACT)


def _memkv(mem2d, w_kv, B, M):
    D = mem2d.shape[1]
    return pl.pallas_call(
        _memkv_kernel,
        grid=(B,),
        in_specs=[pl.BlockSpec((M, D), lambda b: (b, 0)),
                  pl.BlockSpec((D, 2 * MEM_W), lambda b: (0, 0))],
        out_specs=[pl.BlockSpec((None, MEM_W, M), lambda b: (b, 0, 0)),
                   pl.BlockSpec((M, MEM_W), lambda b: (b, 0))],
        out_shape=[jax.ShapeDtypeStruct((B, MEM_W, M), _ACT),
                   jax.ShapeDtypeStruct((B * M, MEM_W), _ACT)],
        compiler_params=pltpu.CompilerParams(vmem_limit_bytes=_VMEM_LIMIT),
        name="memkv",
    )(mem2d, w_kv)


def _inproj_kernel(x_ref, w_ref, wgk_ref, bgk_ref, cos_ref, sin_ref, mkT_ref, mv_ref,
                   gq_ref, gk_ref, gv_ref, gg_ref, la_ref, dqT_ref, dk_ref, dvT_ref, dg_ref, xat_ref):
    tm = x_ref.shape[0]
    xb = x_ref[...].astype(_ACT)

    def proj(base, c0, c1):
        return jnp.dot(xb, w_ref[:, base + c0:base + c1], preferred_element_type=F32)

    gq_ref[...] = (proj(W_OFF_GLA,0, GLA_KW) * (GLA_DK ** -0.5)).astype(_ACT)
    gk_ref[...] = proj(W_OFF_GLA,GLA_KW, 2 * GLA_KW).astype(_ACT)
    gv_ref[...] = proj(W_OFF_GLA,2 * GLA_KW, 2 * GLA_KW + GLA_W).astype(_ACT)
    gg_ref[...] = proj(W_OFF_GLA,2 * GLA_KW + GLA_W, 2 * GLA_KW + 2 * GLA_W).astype(_ACT)

    glr = proj(W_OFF_LR,0, LANES)
    hi, lo = _split_hi_lo(glr)
    logit = jnp.dot(jnp.concatenate([hi, hi, lo], axis=1), wgk_ref[...],
                    preferred_element_type=F32) + bgk_ref[...]
    log_sig = jnp.minimum(logit, 0.0) - jnp.log(1.0 + jnp.exp(-jnp.abs(logit)))
    la_ref[...] = log_sig * (LOG2E / GLA_TAU)

    lane = lax.broadcasted_iota(jnp.int32, (tm, LANES), 1) & (DIFF_DQK - 1)
    cosv = cos_ref[...]
    sinv = sin_ref[...]
    half = ROT_DIM // 2

    def rope(a):
        up = pltpu.roll(a, LANES - half, 1)
        dn = pltpu.roll(a, half, 1)
        return a * cosv + jnp.where(lane < half, -up, dn) * sinv

    dq = proj(W_OFF_DIFF,0, DIFF_W)
    dk = proj(W_OFF_DIFF,DIFF_W, 2 * DIFF_W)
    dv = proj(W_OFF_DIFF,2 * DIFF_W, 3 * DIFF_W)
    ones = jnp.ones((ACT_SUBLANES, tm), _ACT)
    for h in range(DIFF_HEADS):
        sl = slice(h * DIFF_DV, (h + 1) * DIFF_DV)
        dqT_ref[sl, :] = (rope(dq[:, sl]) * (DIFF_DQK ** -0.5 * LOG2E)).T.astype(_ACT)
        dk_ref[:, sl] = rope(dk[:, sl]).astype(_ACT)
        dvT_ref[h * DIFF_VT_ROWS:h * DIFF_VT_ROWS + DIFF_DV, :] = dv[:, sl].T.astype(_ACT)
        dvT_ref[h * DIFF_VT_ROWS + DIFF_DV:(h + 1) * DIFF_VT_ROWS, :] = ones
    dg_ref[...] = proj(W_OFF_DIFF,3 * DIFF_W, 4 * DIFF_W).astype(_ACT)

    mq = proj(W_OFF_MEM,0, MEM_W)
    mg = proj(W_OFF_MEM,MEM_W, 2 * MEM_W)
    for h in range(MEM_HEADS):
        sl = slice(h * MEM_DH, (h + 1) * MEM_DH)
        qh = (mq[:, sl] * (MEM_DH ** -0.5)).astype(_ACT)
        s = jnp.dot(qh, mkT_ref[sl, :], preferred_element_type=F32)
        p = jnp.exp(s - jnp.max(s, axis=-1, keepdims=True))
        l = jnp.sum(p, axis=-1, keepdims=True)
        o = jnp.dot(p.astype(_ACT), mv_ref[:, sl], preferred_element_type=F32) / l
        xat_ref[:, sl] = (o * _silu(mg[:, sl])).astype(_ACT)


def _inproj(x2d, w_all, wgk3, bgk, cos_t, sin_t, mkT, mv, B, S, M, tm):
    T, D = x2d.shape
    ns = S // tm
    row = lambda b, i: (b * ns + i, 0)
    const = lambda b, i: (0, 0)
    whole = lambda a: pl.BlockSpec(a.shape, const, pipeline_mode=pl.Buffered(1))
    rows = lambda w: pl.BlockSpec((tm, w), row)
    tr = pl.BlockSpec((None, DIFF_W, tm), lambda b, i: (b, 0, i))
    trv = pl.BlockSpec((None, DIFF_HEADS * DIFF_VT_ROWS, tm), lambda b, i: (b, 0, i))
    return pl.pallas_call(
        _inproj_kernel,
        grid=(B, ns),
        in_specs=[rows(D), whole(w_all), whole(wgk3), whole(bgk),
                  rows(LANES), rows(LANES),
                  pl.BlockSpec((None, MEM_W, M), lambda b, i: (b, 0, 0)),
                  pl.BlockSpec((M, MEM_W), lambda b, i: (b, 0))],
        out_specs=[rows(GLA_KW), rows(GLA_KW), rows(GLA_W), rows(GLA_W), rows(GLA_KW),
                   tr, rows(DIFF_W), trv, rows(DIFF_W), rows(MEM_W)],
        out_shape=[jax.ShapeDtypeStruct((T, GLA_KW), _ACT), jax.ShapeDtypeStruct((T, GLA_KW), _ACT),
                   jax.ShapeDtypeStruct((T, GLA_W), _ACT), jax.ShapeDtypeStruct((T, GLA_W), _ACT),
                   jax.ShapeDtypeStruct((T, GLA_KW), F32),
                   jax.ShapeDtypeStruct((B, DIFF_W, S), _ACT), jax.ShapeDtypeStruct((T, DIFF_W), _ACT),
                   jax.ShapeDtypeStruct((B, DIFF_HEADS * DIFF_VT_ROWS, S), _ACT),
                   jax.ShapeDtypeStruct((T, DIFF_W), _ACT),
                   jax.ShapeDtypeStruct((T, MEM_W), _ACT)],
        compiler_params=pltpu.CompilerParams(
            dimension_semantics=("arbitrary", "arbitrary"), vmem_limit_bytes=_VMEM_LIMIT),
        name="inproj",
    )(x2d, w_all, wgk3, bgk, cos_t, sin_t, mkT, mv)


def _gla_constants():
    C = GLA_CHUNK
    t = np.arange(C)
    tril = (t[None, :] <= t[:, None]).astype(np.float32)
    rsel = np.zeros((GLA_DIAG * GLA_DK, C), np.float32)
    for j in range(GLA_DIAG):
        rsel[j * GLA_DK:(j + 1) * GLA_DK, j::GLA_DIAG] = 1.0
    return tril, rsel


def _gla_kernel(q_ref, k_ref, v_ref, g_ref, la_ref, tril_ref, rsel_ref, gng_ref, o_ref, st_ref):
    R = q_ref.shape[0]
    C = GLA_CHUNK

    @pl.when(pl.program_id(1) == 0)
    def _():
        st_ref[...] = jnp.zeros_like(st_ref)

    ri = lax.broadcasted_iota(jnp.int32, (C, C), 0)
    ci = lax.broadcasted_iota(jnp.int32, (C, C), 1)
    rk = lax.broadcasted_iota(jnp.int32, (C, GLA_KW), 0)
    pair_mask = [((ri // (2 * s)) == (ci // (2 * s))) & (((ri // s) & 1) == 1) & (((ci // s) & 1) == 0)
                 for s in GLA_LEVELS]
    half_sign = [jnp.where(((rk // s) & 1) == 1, 1.0, -1.0) for s in GLA_LEVELS]
    diag_mask = ((ri // GLA_DIAG) == (ci // GLA_DIAG)) & (ci <= ri)
    groups = C // GLA_DIAG
    tril = tril_ref[...]
    rsel = rsel_ref[...]
    gng = gng_ref[...]

    heads = range(GLA_HEADS)
    sk = [slice(h * GLA_DK, (h + 1) * GLA_DK) for h in heads]
    sv = [slice(h * GLA_DV, (h + 1) * GLA_DV) for h in heads]
    nt = (((1,), (1,)), ((), ()))

    def group_bcast(a, j):
        a3 = a.reshape(groups, GLA_DIAG, GLA_KW)
        return jnp.broadcast_to(a3[:, j:j + 1, :], (groups, GLA_DIAG, GLA_KW)).reshape(C, GLA_KW)

    def chunk(c, carry):
        rows = pl.ds(pl.multiple_of(c * C, C), C)
        q = q_ref[rows, :].astype(F32)
        k = k_ref[rows, :].astype(F32)
        hi, lo = _split_hi_lo(la_ref[rows, :])
        cs = jnp.dot(tril, jnp.concatenate([hi, lo], axis=1), preferred_element_type=F32)
        b = cs[:, :GLA_KW] + cs[:, GLA_KW:]

        a = [jnp.zeros((C, C), F32) for _ in heads]
        for li, s in enumerate(GLA_LEVELS):
            ref = jnp.concatenate(
                [jnp.broadcast_to(b[p + s - 1:p + s, :], (2 * s, GLA_KW)) for p in range(0, C, 2 * s)],
                axis=0)
            x = jnp.exp2(half_sign[li] * (b - ref))
            qx = (q * x).astype(_ACT)
            kx = (k * x).astype(_ACT)
            for h in heads:
                al = lax.dot_general(qx[:, sk[h]], kx[:, sk[h]], nt, preferred_element_type=F32)
                a[h] = jnp.where(pair_mask[li], al, a[h])
        ps = []
        for j in range(GLA_DIAG):
            dec = jnp.exp2(jnp.minimum(b - group_bcast(b, j), 0.0))
            ps.append((q * group_bcast(k, j) * dec).astype(_ACT))
        for h in heads:
            ad = jnp.dot(jnp.concatenate([p[:, sk[h]] for p in ps], axis=1), rsel,
                         preferred_element_type=F32)
            a[h] = jnp.where(diag_mask, ad, a[h])

        qe = (q * jnp.exp2(b)).astype(_ACT)
        b_last = b[C - 1:C, :]
        kd = (k * jnp.exp2(b_last - b)).astype(_ACT)
        keep = jnp.exp2(b_last)
        for h in heads:
            v = v_ref[rows, sv[h]]
            st = st_ref[h]
            o = lax.dot_general(qe[:, sk[h]], st.astype(_ACT), nt, preferred_element_type=F32)
            o = o + jnp.dot(a[h].astype(_ACT), v, preferred_element_type=F32)
            upd = lax.dot_general(v, kd[:, sk[h]], (((0,), (0,)), ((), ())), preferred_element_type=F32)
            st_ref[h] = st * keep[:, sk[h]] + upd

            ms = jnp.mean(o * o, axis=-1, keepdims=True)
            y = o * lax.rsqrt(ms + GLA_NORM_EPS) * gng
            o_ref[rows, sv[h]] = (y * _silu(g_ref[rows, sv[h]].astype(F32))).astype(_ACT)
        return carry

    lax.fori_loop(0, R // C, chunk, 0, unroll=8)


def _gla(gq, gk, gv, gg, la, gng, B, S, R):
    T = gq.shape[0]
    ns = S // R
    tril, rsel = _gla_constants()
    tril = jnp.asarray(tril, _ACT)
    rsel = jnp.asarray(rsel, _ACT)
    row = lambda b, i: (b * ns + i, 0)
    const = lambda b, i: (0, 0)
    rows = lambda w: pl.BlockSpec((R, w), row)
    whole = lambda a: pl.BlockSpec(a.shape, const)
    return pl.pallas_call(
        _gla_kernel,
        grid=(B, ns),
        in_specs=[rows(GLA_KW), rows(GLA_KW), rows(GLA_W), rows(GLA_W), rows(GLA_KW),
                  whole(tril), whole(rsel), whole(gng)],
        out_specs=rows(GLA_W),
        out_shape=jax.ShapeDtypeStruct((T, GLA_W), _ACT),
        scratch_shapes=[pltpu.VMEM((GLA_HEADS, GLA_DV, GLA_DK), F32)],
        compiler_params=pltpu.CompilerParams(
            dimension_semantics=("arbitrary", "arbitrary"), vmem_limit_bytes=_VMEM_LIMIT),
        name="gla",
    )(gq, gk, gv, gg, la, tril, rsel, gng)


def _diff_kernel(qT_ref, qTn_ref, k_ref, vT_ref, g_ref, lq1_ref, lk1_ref, lq2_ref, lk2_ref, gcol_ref, o_ref,
                 s0_ref, s1_ref, bm0_ref, bm1_ref, m_ref, acc_ref, *, lam_init):
    s_refs, bm_refs = (s0_ref, s1_ref), (bm0_ref, bm1_ref)
    tq = qT_ref.shape[1]
    tk = s0_ref.shape[1]
    assert tq == 2 * tk
    qi = pl.program_id(2)
    nfull = 2 * qi
    comp_row = lax.broadcasted_iota(jnp.int32, qT_ref.shape, 0) < DIFF_DQK

    def components(q):
        zero = jnp.zeros_like(q)
        return jnp.where(comp_row, q, zero), jnp.where(comp_row, zero, q)

    qcs = components(qT_ref[...])
    acc_ref[...] = jnp.zeros_like(acc_ref)
    m_ref[...] = jnp.full_like(m_ref, -jnp.inf)

    def produce(ki, slot, c, w, lw=tk, q=None):
        ln = slice(w * lw, (w + 1) * lw)
        kb = k_ref[pl.ds(pl.multiple_of(ki * tk, tk), tk), :]
        s = jnp.dot(kb, (qcs if q is None else q)[c][:, ln], preferred_element_type=F32)
        s_refs[slot][c, :, ln] = s
        bm_refs[slot][c, :, ln] = jnp.max(s, axis=0, keepdims=True)

    def consume(ki, slot, c, w, masked=False, lw=tk):
        ln = slice(w * lw, (w + 1) * lw)
        vb = vT_ref[:, pl.ds(pl.multiple_of(ki * tk, tk), tk)]
        s = s_refs[slot][c, :, ln]
        if masked:
            key = lax.broadcasted_iota(jnp.int32, s.shape, 0)
            qry = lax.broadcasted_iota(jnp.int32, s.shape, 1)
            s = jnp.where(key <= qry, s, -jnp.inf)
            bm = jnp.max(s, axis=0, keepdims=True)
        else:
            bm = bm_refs[slot][c, :, ln]
        m_old = m_ref[c, :, ln]
        m_new = jnp.maximum(m_old, bm)
        p = jnp.exp2(s - m_new).astype(_ACT)
        acc_ref[c, :, ln] = (jnp.exp2(m_old - m_new) * acc_ref[c, :, ln]
                             + jnp.dot(vb, p, preferred_element_type=F32))
        m_ref[c, :, ln] = m_new

    lws = min(_DIFF_LANE_WINDOW, tk)

    def step(k_next, slot_next, k_cur, slot_cur):
        for c in range(2):
            for w in range(tq // lws):
                produce(k_next, slot_next, c, w, lws)
                consume(k_cur, slot_cur, c, w, lw=lws)

    @pl.when(qi == 0)
    def _():
        for c in range(2):
            for w in range(2):
                produce(0, 0, c, w)

    def pair(j):
        step(2 * j + 1, 1, 2 * j, 0)
        step(2 * j + 2, 0, 2 * j + 1, 1)

    def pairs(t, carry):
        for u in range(_DIFF_PAIRS_PER_TRIP):
            pair(_DIFF_PAIRS_PER_TRIP * t + u)
        return carry

    full = qi // _DIFF_PAIRS_PER_TRIP
    lax.fori_loop(0, full, pairs, 0)

    def rest(j, carry):
        pair(j)
        return carry

    lax.fori_loop(full * _DIFF_PAIRS_PER_TRIP, qi, rest, 0)

    qn = components(qTn_ref[...])
    produce(nfull + 1, 1, 0, 1)
    consume(nfull, 0, 0, 0, masked=True)
    consume(nfull, 0, 0, 1)
    produce(nfull + 1, 1, 1, 1)
    consume(nfull, 0, 1, 0, masked=True)
    produce(0, 0, 0, 0, q=qn)
    consume(nfull, 0, 1, 1)
    produce(0, 0, 0, 1, q=qn)
    consume(nfull + 1, 1, 0, 1, masked=True)
    produce(0, 0, 1, 0, q=qn)
    consume(nfull + 1, 1, 1, 1, masked=True)
    produce(0, 0, 1, 1, q=qn)

    lam = (jnp.exp(jnp.sum(lq1_ref[...] * lk1_ref[...], axis=-1, keepdims=True))
           - jnp.exp(jnp.sum(lq2_ref[...] * lk2_ref[...], axis=-1, keepdims=True)) + lam_init)
    l1 = acc_ref[0, DIFF_DV:DIFF_DV + 1, :]
    l2 = acc_ref[1, DIFF_DV:DIFF_DV + 1, :]
    oT = acc_ref[0, 0:DIFF_DV, :] / l1 - lam * (acc_ref[1, 0:DIFF_DV, :] / l2)
    ms = jnp.mean(oT * oT, axis=0, keepdims=True)
    yT = oT * lax.rsqrt(ms + DIFF_NORM_EPS) * (gcol_ref[...] * (1.0 - lam_init))
    o_ref[...] = (yT.T * _silu(g_ref[...].astype(F32))).astype(_ACT)


def _diff(dqT, dk, dvT, dg, lq1, lk1, lq2, lk2, gcol, B, S, tq, lam_init):
    T = dk.shape[0]
    nq = S // tq
    small = lambda a: pl.BlockSpec(a.shape, lambda b, h, i: (0, 0))
    blk = pl.BlockSpec((tq, DIFF_DV), lambda b, h, i: (b * nq + i, h))
    return pl.pallas_call(
        functools.partial(_diff_kernel, lam_init=lam_init),
        grid=(B, DIFF_HEADS, nq),
        in_specs=[pl.BlockSpec((None, DIFF_DV, tq), lambda b, h, i: (b, h, i)),
                  pl.BlockSpec((None, DIFF_DV, tq), lambda b, h, i: (b, h, jnp.minimum(i + 1, nq - 1))),
                  pl.BlockSpec((S, DIFF_DV), lambda b, h, i: (b, h)),
                  pl.BlockSpec((None, DIFF_VT_ROWS, S), lambda b, h, i: (b, h, 0)),
                  blk, small(lq1), small(lk1), small(lq2), small(lk2), small(gcol)],
        out_specs=blk,
        out_shape=jax.ShapeDtypeStruct((T, DIFF_W), _ACT),
        scratch_shapes=[pltpu.VMEM((2, tq // 2, tq), F32), pltpu.VMEM((2, tq // 2, tq), F32),
                        pltpu.VMEM((2, 1, tq), F32), pltpu.VMEM((2, 1, tq), F32),
                        pltpu.VMEM((2, 1, tq), F32), pltpu.VMEM((2, DIFF_VT_ROWS, tq), F32)],
        compiler_params=pltpu.CompilerParams(
            dimension_semantics=("arbitrary", "arbitrary", "arbitrary"), vmem_limit_bytes=_VMEM_LIMIT),
        name="diffattn",
    )(dqT, dqT, dk, dvT, dg, lq1, lk1, lq2, lk2, gcol)


def _outproj_kernel(gla_ref, dif_ref, xat_ref, x_ref, w_ref, g_ref, b_ref, o_ref, *, alpha):
    y = alpha * x_ref[...]
    y = y + jnp.dot(gla_ref[...], w_ref[0:GLA_W, :], preferred_element_type=F32)
    y = y + jnp.dot(dif_ref[...], w_ref[GLA_W:GLA_W + DIFF_W, :], preferred_element_type=F32)
    y = y + jnp.dot(xat_ref[...], w_ref[GLA_W + DIFF_W:GLA_W + DIFF_W + MEM_W, :], preferred_element_type=F32)
    mu = jnp.mean(y, axis=-1, keepdims=True)
    yc = y - mu
    var = jnp.mean(yc * yc, axis=-1, keepdims=True)
    o_ref[...] = yc * lax.rsqrt(var + LN_EPS) * g_ref[...] + b_ref[...]


def _outproj(gla, dif, xat, x2d, w_out, ln_g, ln_b, tm, alpha):
    T, D = x2d.shape
    row = lambda i: (i, 0)
    const = lambda i: (0, 0)
    rows = lambda w: pl.BlockSpec((tm, w), row)
    return pl.pallas_call(
        functools.partial(_outproj_kernel, alpha=alpha),
        grid=(T // tm,),
        in_specs=[rows(GLA_W), rows(DIFF_W), rows(MEM_W), rows(D),
                  pl.BlockSpec(w_out.shape, const, pipeline_mode=pl.Buffered(1)),
                  pl.BlockSpec(ln_g.shape, const), pl.BlockSpec(ln_b.shape, const)],
        out_specs=rows(D),
        out_shape=jax.ShapeDtypeStruct((T, D), F32),
        compiler_params=pltpu.CompilerParams(
            dimension_semantics=("arbitrary",), vmem_limit_bytes=_VMEM_LIMIT),
        name="outproj",
    )(gla, dif, xat, x2d, w_out, ln_g, ln_b)


def _rotary_tables(positions):
    B, S = positions.shape
    half = ROT_DIM // 2
    inv_freq = jnp.power(jnp.float32(ROPE_THETA), -(jnp.arange(0, ROT_DIM, 2, dtype=F32) / ROT_DIM))
    lane = np.arange(LANES) % DIFF_DQK
    lane_freq = jnp.where(lane < ROT_DIM, inv_freq[lane % half], 0.0)
    ang = positions.astype(F32).reshape(B * S, 1) * lane_freq[None, :]
    return jnp.cos(ang), jnp.sin(ang)


def _tile(default, n):
    t = min(default, n)
    assert n % t == 0, (n, t)
    return t


def kernel(x, mem, positions, w_in, w_gk_up, b_gk_up, gla_norm_g, lambda_q1, lambda_k1, lambda_q2,
           lambda_k2, diff_norm_g, w_mem_kv, w_out, ln_g, ln_b):
    B, S, D = x.shape
    M = mem.shape[1]
    depth = w_in.shape[0]
    T = B * S
    assert S % GLA_CHUNK == 0 and M % LANES == 0
    tm_in, r_gla, tq, tm_out = _tile(_TM_IN, S), _tile(_R_GLA, S), _tile(_TQ, S), _tile(_TM_OUT, S)
    alpha = (2.0 * depth) ** 0.25

    cos_t, sin_t = _rotary_tables(positions)
    mem2d = mem.reshape(B * M, D)
    h = x.reshape(T, D)

    o_lr = 2 * GLA_KW + 2 * GLA_W
    o_d = o_lr + GLA_LOWRANK
    o_m = o_d + 4 * DIFF_W
    for l in range(depth):
        w = w_in[l]
        w_all = jnp.concatenate(
            [w[:, :o_lr], w[:, o_d:], w[:, o_lr:o_d], jnp.zeros((D, LANES - GLA_LOWRANK), w.dtype)],
            axis=1).astype(_ACT)
        wgk = jnp.pad(w_gk_up[l].astype(F32), ((0, LANES - GLA_LOWRANK), (0, 0)))
        wgk_hi, wgk_lo = _split_hi_lo(wgk)
        wgk3 = jnp.concatenate([wgk_hi, wgk_lo, wgk_hi], axis=0)
        bgk = b_gk_up[l].astype(F32).reshape(1, GLA_KW)

        mkT, mv = _memkv(mem2d, w_mem_kv[l].astype(_ACT), B, M)
        gq, gk, gv, gg, la, dqT, dk, dvT, dg, xat = _inproj(
            h, w_all, wgk3, bgk, cos_t, sin_t, mkT, mv, B, S, M, tm_in)
        gla = _gla(gq, gk, gv, gg, la, gla_norm_g[l].astype(F32).reshape(1, GLA_DV), B, S, r_gla)
        row64 = lambda a: a[l].astype(F32).reshape(1, DIFF_DQK)
        dif = _diff(dqT, dk, dvT, dg, row64(lambda_q1), row64(lambda_k1), row64(lambda_q2),
                    row64(lambda_k2), diff_norm_g[l].astype(F32).reshape(DIFF_DV, 1), B, S, tq,
                    _lambda_init(l))
        h = _outproj(gla, dif, xat, h, w_out[l].astype(_ACT), ln_g[l].astype(F32).reshape(1, D),
                     ln_b[l].astype(F32).reshape(1, D), tm_out, alpha)
    return h.reshape(B, S, D)
```

```python
import functools
import math

import jax
import jax.numpy as jnp
import numpy as np
from jax import lax
from jax.experimental import pallas as pl
from jax.experimental.pallas import tpu as pltpu

F32 = jnp.float32
_ACT = jnp.bfloat16

GLA_HEADS = 4
GLA_DK = 128
GLA_DV = 256
GLA_LOWRANK = 16
GLA_TAU = 16.0
GLA_NORM_EPS = 1e-6
GLA_W = GLA_HEADS * GLA_DV
GLA_KW = GLA_HEADS * GLA_DK

DIFF_HEADS = 4
DIFF_DV = 128
DIFF_DQK = 64
DIFF_NORM_EPS = 1e-5
DIFF_W = DIFF_HEADS * DIFF_DV
ACT_SUBLANES = 16
DIFF_VT_ROWS = DIFF_DV + ACT_SUBLANES
LOG2E = math.log2(math.e)

MEM_HEADS = 4
MEM_DH = 128
MEM_W = MEM_HEADS * MEM_DH

ROPE_THETA = 500000.0
ROT_DIM = DIFF_DQK // 4
LN_EPS = 1e-5

LANES = 128
SUBLANES = 8

W_OFF_GLA = 0
W_OFF_DIFF = W_OFF_GLA + 2 * GLA_KW + 2 * GLA_W
W_OFF_MEM = W_OFF_DIFF + 4 * DIFF_W
W_OFF_LR = W_OFF_MEM + 2 * MEM_W
W_COLS = W_OFF_LR + LANES

GLA_CHUNK = 64
GLA_LEVELS = (32, 16, 8)
GLA_DIAG = 8

_TM_IN = 512
_R_GLA = 1024
_TQ = 1024
_TM_OUT = 512
_DIFF_LANE_WINDOW = 256
_DIFF_PAIRS_PER_TRIP = 4
_VMEM_LIMIT = 60 * 1024 * 1024


def _lambda_init(layer):
    return 0.8 - 0.6 * math.exp(-0.3 * layer)


def _silu(g):
    half = 0.5 * g
    return half + half * jnp.tanh(half)


def _split_hi_lo(a):
    hi = a.astype(_ACT)
    lo = (a - hi.astype(F32)).astype(_ACT)
    return hi, lo


def _memkv_kernel(mem_ref, w_ref, mkT_ref, mv_ref):
    kv = jnp.dot(mem_ref[...].astype(_ACT), w_ref[...], preferred_element_type=F32)
    mkT_ref[...] = kv[:, :MEM_W].T.astype(_ACT)
    mv_ref[...] = kv[:, MEM_W:].astype(_ACT)


def _memkv(mem2d, w_kv, B, M):
    D = mem2d.shape[1]
    return pl.pallas_call(
        _memkv_kernel,
        grid=(B,),
        in_specs=[pl.BlockSpec((M, D), lambda b: (b, 0)),
                  pl.BlockSpec((D, 2 * MEM_W), lambda b: (0, 0))],
        out_specs=[pl.BlockSpec((None, MEM_W, M), lambda b: (b, 0, 0)),
                   pl.BlockSpec((M, MEM_W), lambda b: (b, 0))],
        out_shape=[jax.ShapeDtypeStruct((B, MEM_W, M), _ACT),
                   jax.ShapeDtypeStruct((B * M, MEM_W), _ACT)],
        compiler_params=pltpu.CompilerParams(vmem_limit_bytes=_VMEM_LIMIT),
        name="memkv",
    )(mem2d, w_kv)


def _inproj_kernel(x_ref, w_ref, wgk_ref, bgk_ref, cos_ref, sin_ref, mkT_ref, mv_ref,
                   gq_ref, gk_ref, gv_ref, gg_ref, la_ref, dqT_ref, dk_ref, dvT_ref, dg_ref, xat_ref):
    tm = x_ref.shape[0]
    xb = x_ref[...].astype(_ACT)

    def proj(base, c0, c1):
        return jnp.dot(xb, w_ref[:, base + c0:base + c1], preferred_element_type=F32)

    gq_ref[...] = (proj(W_OFF_GLA,0, GLA_KW) * (GLA_DK ** -0.5)).astype(_ACT)
    gk_ref[...] = proj(W_OFF_GLA,GLA_KW, 2 * GLA_KW).astype(_ACT)
    gv_ref[...] = proj(W_OFF_GLA,2 * GLA_KW, 2 * GLA_KW + GLA_W).astype(_ACT)
    gg_ref[...] = proj(W_OFF_GLA,2 * GLA_KW + GLA_W, 2 * GLA_KW + 2 * GLA_W).astype(_ACT)

    glr = proj(W_OFF_LR,0, LANES)
    hi, lo = _split_hi_lo(glr)
    logit = jnp.dot(jnp.concatenate([hi, hi, lo], axis=1), wgk_ref[...],
                    preferred_element_type=F32) + bgk_ref[...]
    log_sig = jnp.minimum(logit, 0.0) - jnp.log(1.0 + jnp.exp(-jnp.abs(logit)))
    la_ref[...] = log_sig * (LOG2E / GLA_TAU)

    lane = lax.broadcasted_iota(jnp.int32, (tm, LANES), 1) & (DIFF_DQK - 1)
    cosv = cos_ref[...]
    sinv = sin_ref[...]
    half = ROT_DIM // 2

    def rope(a):
        up = pltpu.roll(a, LANES - half, 1)
        dn = pltpu.roll(a, half, 1)
        return a * cosv + jnp.where(lane < half, -up, dn) * sinv

    dq = proj(W_OFF_DIFF,0, DIFF_W)
    dk = proj(W_OFF_DIFF,DIFF_W, 2 * DIFF_W)
    dv = proj(W_OFF_DIFF,2 * DIFF_W, 3 * DIFF_W)
    ones = jnp.ones((ACT_SUBLANES, tm), _ACT)
    for h in range(DIFF_HEADS):
        sl = slice(h * DIFF_DV, (h + 1) * DIFF_DV)
        dqT_ref[sl, :] = (rope(dq[:, sl]) * (DIFF_DQK ** -0.5 * LOG2E)).T.astype(_ACT)
        dk_ref[:, sl] = rope(dk[:, sl]).astype(_ACT)
        dvT_ref[h * DIFF_VT_ROWS:h * DIFF_VT_ROWS + DIFF_DV, :] = dv[:, sl].T.astype(_ACT)
        dvT_ref[h * DIFF_VT_ROWS + DIFF_DV:(h + 1) * DIFF_VT_ROWS, :] = ones
    dg_ref[...] = proj(W_OFF_DIFF,3 * DIFF_W, 4 * DIFF_W).astype(_ACT)

    mq = proj(W_OFF_MEM,0, MEM_W)
    mg = proj(W_OFF_MEM,MEM_W, 2 * MEM_W)
    for h in range(MEM_HEADS):
        sl = slice(h * MEM_DH, (h + 1) * MEM_DH)
        qh = (mq[:, sl] * (MEM_DH ** -0.5)).astype(_ACT)
        s = jnp.dot(qh, mkT_ref[sl, :], preferred_element_type=F32)
        p = jnp.exp(s - jnp.max(s, axis=-1, keepdims=True))
        l = jnp.sum(p, axis=-1, keepdims=True)
        o = jnp.dot(p.astype(_ACT), mv_ref[:, sl], preferred_element_type=F32) / l
        xat_ref[:, sl] = (o * _silu(mg[:, sl])).astype(_ACT)


def _inproj(x2d, w_all, wgk3, bgk, cos_t, sin_t, mkT, mv, B, S, M, tm):
    T, D = x2d.shape
    ns = S // tm
    row = lambda b, i: (b * ns + i, 0)
    const = lambda b, i: (0, 0)
    whole = lambda a: pl.BlockSpec(a.shape, const, pipeline_mode=pl.Buffered(1))
    rows = lambda w: pl.BlockSpec((tm, w), row)
    tr = pl.BlockSpec((None, DIFF_W, tm), lambda b, i: (b, 0, i))
    trv = pl.BlockSpec((None, DIFF_HEADS * DIFF_VT_ROWS, tm), lambda b, i: (b, 0, i))
    return pl.pallas_call(
        _inproj_kernel,
        grid=(B, ns),
        in_specs=[rows(D), whole(w_all), whole(wgk3), whole(bgk),
                  rows(LANES), rows(LANES),
                  pl.BlockSpec((None, MEM_W, M), lambda b, i: (b, 0, 0)),
                  pl.BlockSpec((M, MEM_W), lambda b, i: (b, 0))],
        out_specs=[rows(GLA_KW), rows(GLA_KW), rows(GLA_W), rows(GLA_W), rows(GLA_KW),
                   tr, rows(DIFF_W), trv, rows(DIFF_W), rows(MEM_W)],
        out_shape=[jax.ShapeDtypeStruct((T, GLA_KW), _ACT), jax.ShapeDtypeStruct((T, GLA_KW), _ACT),
                   jax.ShapeDtypeStruct((T, GLA_W), _ACT), jax.ShapeDtypeStruct((T, GLA_W), _ACT),
                   jax.ShapeDtypeStruct((T, GLA_KW), F32),
                   jax.ShapeDtypeStruct((B, DIFF_W, S), _ACT), jax.ShapeDtypeStruct((T, DIFF_W), _ACT),
                   jax.ShapeDtypeStruct((B, DIFF_HEADS * DIFF_VT_ROWS, S), _ACT),
                   jax.ShapeDtypeStruct((T, DIFF_W), _ACT),
                   jax.ShapeDtypeStruct((T, MEM_W), _ACT)],
        compiler_params=pltpu.CompilerParams(
            dimension_semantics=("arbitrary", "arbitrary"), vmem_limit_bytes=_VMEM_LIMIT),
        name="inproj",
    )(x2d, w_all, wgk3, bgk, cos_t, sin_t, mkT, mv)


def _gla_constants():
    C = GLA_CHUNK
    t = np.arange(C)
    tril = (t[None, :] <= t[:, None]).astype(np.float32)
    rsel = np.zeros((GLA_DIAG * GLA_DK, C), np.float32)
    for j in range(GLA_DIAG):
        rsel[j * GLA_DK:(j + 1) * GLA_DK, j::GLA_DIAG] = 1.0
    return tril, rsel


def _gla_kernel(q_ref, k_ref, v_ref, g_ref, la_ref, tril_ref, rsel_ref, gng_ref, o_ref, st_ref):
    R = q_ref.shape[0]
    C = GLA_CHUNK

    @pl.when(pl.program_id(1) == 0)
    def _():
        st_ref[...] = jnp.zeros_like(st_ref)

    ri = lax.broadcasted_iota(jnp.int32, (C, C), 0)
    ci = lax.broadcasted_iota(jnp.int32, (C, C), 1)
    rk = lax.broadcasted_iota(jnp.int32, (C, GLA_KW), 0)
    pair_mask = [((ri // (2 * s)) == (ci // (2 * s))) & (((ri // s) & 1) == 1) & (((ci // s) & 1) == 0)
                 for s in GLA_LEVELS]
    half_sign = [jnp.where(((rk // s) & 1) == 1, 1.0, -1.0) for s in GLA_LEVELS]
    diag_mask = ((ri // GLA_DIAG) == (ci // GLA_DIAG)) & (ci <= ri)
    groups = C // GLA_DIAG
    tril = tril_ref[...]
    rsel = rsel_ref[...]
    gng = gng_ref[...]

    heads = range(GLA_HEADS)
    sk = [slice(h * GLA_DK, (h + 1) * GLA_DK) for h in heads]
    sv = [slice(h * GLA_DV, (h + 1) * GLA_DV) for h in heads]
    nt = (((1,), (1,)), ((), ()))

    def group_bcast(a, j):
        a3 = a.reshape(groups, GLA_DIAG, GLA_KW)
        return jnp.broadcast_to(a3[:, j:j + 1, :], (groups, GLA_DIAG, GLA_KW)).reshape(C, GLA_KW)

    def chunk(c, carry):
        rows = pl.ds(pl.multiple_of(c * C, C), C)
        q = q_ref[rows, :].astype(F32)
        k = k_ref[rows, :].astype(F32)
        hi, lo = _split_hi_lo(la_ref[rows, :])
        cs = jnp.dot(tril, jnp.concatenate([hi, lo], axis=1), preferred_element_type=F32)
        b = cs[:, :GLA_KW] + cs[:, GLA_KW:]

        a = [jnp.zeros((C, C), F32) for _ in heads]
        for li, s in enumerate(GLA_LEVELS):
            ref = jnp.concatenate(
                [jnp.broadcast_to(b[p + s - 1:p + s, :], (2 * s, GLA_KW)) for p in range(0, C, 2 * s)],
                axis=0)
            x = jnp.exp2(half_sign[li] * (b - ref))
            qx = (q * x).astype(_ACT)
            kx = (k * x).astype(_ACT)
            for h in heads:
                al = lax.dot_general(qx[:, sk[h]], kx[:, sk[h]], nt, preferred_element_type=F32)
                a[h] = jnp.where(pair_mask[li], al, a[h])
        ps = []
        for j in range(GLA_DIAG):
            dec = jnp.exp2(jnp.minimum(b - group_bcast(b, j), 0.0))
            ps.append((q * group_bcast(k, j) * dec).astype(_ACT))
        for h in heads:
            ad = jnp.dot(jnp.concatenate([p[:, sk[h]] for p in ps], axis=1), rsel,
                         preferred_element_type=F32)
            a[h] = jnp.where(diag_mask, ad, a[h])

        qe = (q * jnp.exp2(b)).astype(_ACT)
        b_last = b[C - 1:C, :]
        kd = (k * jnp.exp2(b_last - b)).astype(_ACT)
        keep = jnp.exp2(b_last)
        for h in heads:
            v = v_ref[rows, sv[h]]
            st = st_ref[h]
            o = lax.dot_general(qe[:, sk[h]], st.astype(_ACT), nt, preferred_element_type=F32)
            o = o + jnp.dot(a[h].astype(_ACT), v, preferred_element_type=F32)
            upd = lax.dot_general(v, kd[:, sk[h]], (((0,), (0,)), ((), ())), preferred_element_type=F32)
            st_ref[h] = st * keep[:, sk[h]] + upd

            ms = jnp.mean(o * o, axis=-1, keepdims=True)
            y = o * lax.rsqrt(ms + GLA_NORM_EPS) * gng
            o_ref[rows, sv[h]] = (y * _silu(g_ref[rows, sv[h]].astype(F32))).astype(_ACT)
        return carry

    lax.fori_loop(0, R // C, chunk, 0, unroll=8)


def _gla(gq, gk, gv, gg, la, gng, B, S, R):
    T = gq.shape[0]
    ns = S // R
    tril, rsel = _gla_constants()
    tril = jnp.asarray(tril, _ACT)
    rsel = jnp.asarray(rsel, _ACT)
    row = lambda b, i: (b * ns + i, 0)
    const = lambda b, i: (0, 0)
    rows = lambda w: pl.BlockSpec((R, w), row)
    whole = lambda a: pl.BlockSpec(a.shape, const)
    return pl.pallas_call(
        _gla_kernel,
        grid=(B, ns),
        in_specs=[rows(GLA_KW), rows(GLA_KW), rows(GLA_W), rows(GLA_W), rows(GLA_KW),
                  whole(tril), whole(rsel), whole(gng)],
        out_specs=rows(GLA_W),
        out_shape=jax.ShapeDtypeStruct((T, GLA_W), _ACT),
        scratch_shapes=[pltpu.VMEM((GLA_HEADS, GLA_DV, GLA_DK), F32)],
        compiler_params=pltpu.CompilerParams(
            dimension_semantics=("arbitrary", "arbitrary"), vmem_limit_bytes=_VMEM_LIMIT),
        name="gla",
    )(gq, gk, gv, gg, la, tril, rsel, gng)


def _diff_kernel(qT_ref, qTn_ref, k_ref, vT_ref, g_ref, lq1_ref, lk1_ref, lq2_ref, lk2_ref, gcol_ref, o_ref,
                 s0_ref, s1_ref, bm0_ref, bm1_ref, m_ref, acc_ref, *, lam_init):
    s_refs, bm_refs = (s0_ref, s1_ref), (bm0_ref, bm1_ref)
    tq = qT_ref.shape[1]
    tk = s0_ref.shape[1]
    assert tq == 2 * tk
    qi = pl.program_id(2)
    nfull = 2 * qi
    comp_row = lax.broadcasted_iota(jnp.int32, qT_ref.shape, 0) < DIFF_DQK

    def components(q):
        zero = jnp.zeros_like(q)
        return jnp.where(comp_row, q, zero), jnp.where(comp_row, zero, q)

    qcs = components(qT_ref[...])
    acc_ref[...] = jnp.zeros_like(acc_ref)
    m_ref[...] = jnp.full_like(m_ref, -jnp.inf)

    def produce(ki, slot, c, w, lw=tk, q=None):
        ln = slice(w * lw, (w + 1) * lw)
        kb = k_ref[pl.ds(pl.multiple_of(ki * tk, tk), tk), :]
        s = jnp.dot(kb, (qcs if q is None else q)[c][:, ln], preferred_element_type=F32)
        s_refs[slot][c, :, ln] = s
        bm_refs[slot][c, :, ln] = jnp.max(s, axis=0, keepdims=True)

    def consume(ki, slot, c, w, masked=False, lw=tk):
        ln = slice(w * lw, (w + 1) * lw)
        vb = vT_ref[:, pl.ds(pl.multiple_of(ki * tk, tk), tk)]
        s = s_refs[slot][c, :, ln]
        if masked:
            key = lax.broadcasted_iota(jnp.int32, s.shape, 0)
            qry = lax.broadcasted_iota(jnp.int32, s.shape, 1)
            s = jnp.where(key <= qry, s, -jnp.inf)
            bm = jnp.max(s, axis=0, keepdims=True)
        else:
            bm = bm_refs[slot][c, :, ln]
        m_old = m_ref[c, :, ln]
        m_new = jnp.maximum(m_old, bm)
        p = jnp.exp2(s - m_new).astype(_ACT)
        acc_ref[c, :, ln] = (jnp.exp2(m_old - m_new) * acc_ref[c, :, ln]
                             + jnp.dot(vb, p, preferred_element_type=F32))
        m_ref[c, :, ln] = m_new

    lws = min(_DIFF_LANE_WINDOW, tk)

    def step(k_next, slot_next, k_cur, slot_cur):
        for c in range(2):
            for w in range(tq // lws):
                produce(k_next, slot_next, c, w, lws)
                consume(k_cur, slot_cur, c, w, lw=lws)

    @pl.when(qi == 0)
    def _():
        for c in range(2):
            for w in range(2):
                produce(0, 0, c, w)

    def pair(j):
        step(2 * j + 1, 1, 2 * j, 0)
        step(2 * j + 2, 0, 2 * j + 1, 1)

    def pairs(t, carry):
        for u in range(_DIFF_PAIRS_PER_TRIP):
            pair(_DIFF_PAIRS_PER_TRIP * t + u)
        return carry

    assert _DIFF_PAIRS_PER_TRIP == 4
    full = qi // 4
    lax.fori_loop(0, full, pairs, 0)
    rem = qi - 4 * full

    @pl.when(rem >= 2)
    def _():
        pair(4 * full)
        pair(4 * full + 1)

    @pl.when(rem % 2 == 1)
    def _():
        pair(qi - 1)

    qn = components(qTn_ref[...])
    produce(nfull + 1, 1, 0, 1)
    consume(nfull, 0, 0, 0, masked=True)
    consume(nfull, 0, 0, 1)
    produce(nfull + 1, 1, 1, 1)
    consume(nfull, 0, 1, 0, masked=True)
    produce(0, 0, 0, 0, q=qn)
    consume(nfull, 0, 1, 1)
    produce(0, 0, 0, 1, q=qn)
    consume(nfull + 1, 1, 0, 1, masked=True)
    produce(0, 0, 1, 0, q=qn)
    consume(nfull + 1, 1, 1, 1, masked=True)
    produce(0, 0, 1, 1, q=qn)

    lam = (jnp.exp(jnp.sum(lq1_ref[...] * lk1_ref[...], axis=-1, keepdims=True))
           - jnp.exp(jnp.sum(lq2_ref[...] * lk2_ref[...], axis=-1, keepdims=True)) + lam_init)
    l1 = acc_ref[0, DIFF_DV:DIFF_DV + 1, :]
    l2 = acc_ref[1, DIFF_DV:DIFF_DV + 1, :]
    oT = acc_ref[0, 0:DIFF_DV, :] / l1 - lam * (acc_ref[1, 0:DIFF_DV, :] / l2)
    ms = jnp.mean(oT * oT, axis=0, keepdims=True)
    yT = oT * lax.rsqrt(ms + DIFF_NORM_EPS) * (gcol_ref[...] * (1.0 - lam_init))
    o_ref[...] = (yT.T * _silu(g_ref[...].astype(F32))).astype(_ACT)


def _diff(dqT, dk, dvT, dg, lq1, lk1, lq2, lk2, gcol, B, S, tq, lam_init):
    T = dk.shape[0]
    nq = S // tq
    small = lambda a: pl.BlockSpec(a.shape, lambda b, h, i: (0, 0))
    blk = pl.BlockSpec((tq, DIFF_DV), lambda b, h, i: (b * nq + i, h))
    return pl.pallas_call(
        functools.partial(_diff_kernel, lam_init=lam_init),
        grid=(B, DIFF_HEADS, nq),
        in_specs=[pl.BlockSpec((None, DIFF_DV, tq), lambda b, h, i: (b, h, i)),
                  pl.BlockSpec((None, DIFF_DV, tq), lambda b, h, i: (b, h, jnp.minimum(i + 1, nq - 1))),
                  pl.BlockSpec((S, DIFF_DV), lambda b, h, i: (b, h)),
                  pl.BlockSpec((None, DIFF_VT_ROWS, S), lambda b, h, i: (b, h, 0)),
                  blk, small(lq1), small(lk1), small(lq2), small(lk2), small(gcol)],
        out_specs=blk,
        out_shape=jax.ShapeDtypeStruct((T, DIFF_W), _ACT),
        scratch_shapes=[pltpu.VMEM((2, tq // 2, tq), F32), pltpu.VMEM((2, tq // 2, tq), F32),
                        pltpu.VMEM((2, 1, tq), F32), pltpu.VMEM((2, 1, tq), F32),
                        pltpu.VMEM((2, 1, tq), F32), pltpu.VMEM((2, DIFF_VT_ROWS, tq), F32)],
        compiler_params=pltpu.CompilerParams(
            dimension_semantics=("arbitrary", "arbitrary", "arbitrary"), vmem_limit_bytes=_VMEM_LIMIT),
        name="diffattn",
    )(dqT, dqT, dk, dvT, dg, lq1, lk1, lq2, lk2, gcol)


def _outproj_kernel(gla_ref, dif_ref, xat_ref, x_ref, w_ref, g_ref, b_ref, o_ref, *, alpha):
    y = alpha * x_ref[...]
    y = y + jnp.dot(gla_ref[...], w_ref[0:GLA_W, :], preferred_element_type=F32)
    y = y + jnp.dot(dif_ref[...], w_ref[GLA_W:GLA_W + DIFF_W, :], preferred_element_type=F32)
    y = y + jnp.dot(xat_ref[...], w_ref[GLA_W + DIFF_W:GLA_W + DIFF_W + MEM_W, :], preferred_element_type=F32)
    mu = jnp.mean(y, axis=-1, keepdims=True)
    yc = y - mu
    var = jnp.mean(yc * yc, axis=-1, keepdims=True)
    o_ref[...] = yc * lax.rsqrt(var + LN_EPS) * g_ref[...] + b_ref[...]


def _outproj(gla, dif, xat, x2d, w_out, ln_g, ln_b, tm, alpha):
    T, D = x2d.shape
    row = lambda i: (i, 0)
    const = lambda i: (0, 0)
    rows = lambda w: pl.BlockSpec((tm, w), row)
    return pl.pallas_call(
        functools.partial(_outproj_kernel, alpha=alpha),
        grid=(T // tm,),
        in_specs=[rows(GLA_W), rows(DIFF_W), rows(MEM_W), rows(D),
                  pl.BlockSpec(w_out.shape, const, pipeline_mode=pl.Buffered(1)),
                  pl.BlockSpec(ln_g.shape, const), pl.BlockSpec(ln_b.shape, const)],
        out_specs=rows(D),
        out_shape=jax.ShapeDtypeStruct((T, D), F32),
        compiler_params=pltpu.CompilerParams(
            dimension_semantics=("arbitrary",), vmem_limit_bytes=_VMEM_LIMIT),
        name="outproj",
    )(gla, dif, xat, x2d, w_out, ln_g, ln_b)


def _rotary_tables(positions):
    B, S = positions.shape
    half = ROT_DIM // 2
    inv_freq = jnp.power(jnp.float32(ROPE_THETA), -(jnp.arange(0, ROT_DIM, 2, dtype=F32) / ROT_DIM))
    lane = np.arange(LANES) % DIFF_DQK
    lane_freq = jnp.where(lane < ROT_DIM, inv_freq[lane % half], 0.0)
    ang = positions.astype(F32).reshape(B * S, 1) * lane_freq[None, :]
    return jnp.cos(ang), jnp.sin(ang)


def _tile(default, n):
    t = min(default, n)
    assert n % t == 0, (n, t)
    return t


def kernel(x, mem, positions, w_in, w_gk_up, b_gk_up, gla_norm_g, lambda_q1, lambda_k1, lambda_q2,
           lambda_k2, diff_norm_g, w_mem_kv, w_out, ln_g, ln_b):
    B, S, D = x.shape
    M = mem.shape[1]
    depth = w_in.shape[0]
    T = B * S
    assert S % GLA_CHUNK == 0 and M % LANES == 0
    tm_in, r_gla, tq, tm_out = _tile(_TM_IN, S), _tile(_R_GLA, S), _tile(_TQ, S), _tile(_TM_OUT, S)
    alpha = (2.0 * depth) ** 0.25

    cos_t, sin_t = _rotary_tables(positions)
    mem2d = mem.reshape(B * M, D)
    h = x.reshape(T, D)

    o_lr = 2 * GLA_KW + 2 * GLA_W
    o_d = o_lr + GLA_LOWRANK
    o_m = o_d + 4 * DIFF_W
    for l in range(depth):
        w = w_in[l]
        w_all = jnp.concatenate(
            [w[:, :o_lr], w[:, o_d:], w[:, o_lr:o_d], jnp.zeros((D, LANES - GLA_LOWRANK), w.dtype)],
            axis=1).astype(_ACT)
        wgk = jnp.pad(w_gk_up[l].astype(F32), ((0, LANES - GLA_LOWRANK), (0, 0)))
        wgk_hi, wgk_lo = _split_hi_lo(wgk)
        wgk3 = jnp.concatenate([wgk_hi, wgk_lo, wgk_hi], axis=0)
        bgk = b_gk_up[l].astype(F32).reshape(1, GLA_KW)

        mkT, mv = _memkv(mem2d, w_mem_kv[l].astype(_ACT), B, M)
        gq, gk, gv, gg, la, dqT, dk, dvT, dg, xat = _inproj(
            h, w_all, wgk3, bgk, cos_t, sin_t, mkT, mv, B, S, M, tm_in)
        gla = _gla(gq, gk, gv, gg, la, gla_norm_g[l].astype(F32).reshape(1, GLA_DV), B, S, r_gla)
        row64 = lambda a: a[l].astype(F32).reshape(1, DIFF_DQK)
        dif = _diff(dqT, dk, dvT, dg, row64(lambda_q1), row64(lambda_k1), row64(lambda_q2),
                    row64(lambda_k2), diff_norm_g[l].astype(F32).reshape(DIFF_DV, 1), B, S, tq,
                    _lambda_init(l))
        h = _outproj(gla, dif, xat, h, w_out[l].astype(_ACT), ln_g[l].astype(F32).reshape(1, D),
                     ln_b[l].astype(F32).reshape(1, D), tm_out, alpha)
    return h.reshape(B, S, D)
```

```python
import functools
import math

import jax
import jax.numpy as jnp
import numpy as np
from jax import lax
from jax.experimental import pallas as pl
from jax.experimental.pallas import tpu as pltpu

F32 = jnp.float32
_ACT = jnp.bfloat16

GLA_HEADS = 4
GLA_DK = 128
GLA_DV = 256
GLA_LOWRANK = 16
GLA_TAU = 16.0
GLA_NORM_EPS = 1e-6
GLA_W = GLA_HEADS * GLA_DV
GLA_KW = GLA_HEADS * GLA_DK

DIFF_HEADS = 4
DIFF_DV = 128
DIFF_DQK = 64
DIFF_NORM_EPS = 1e-5
DIFF_W = DIFF_HEADS * DIFF_DV
ACT_SUBLANES = 16
DIFF_VT_ROWS = DIFF_DV + ACT_SUBLANES
LOG2E = math.log2(math.e)

MEM_HEADS = 4
MEM_DH = 128
MEM_W = MEM_HEADS * MEM_DH

ROPE_THETA = 500000.0
ROT_DIM = DIFF_DQK // 4
LN_EPS = 1e-5

LANES = 128
SUBLANES = 8

W_OFF_GLA = 0
W_OFF_DIFF = W_OFF_GLA + 2 * GLA_KW + 2 * GLA_W
W_OFF_MEM = W_OFF_DIFF + 4 * DIFF_W
W_OFF_LR = W_OFF_MEM + 2 * MEM_W
W_COLS = W_OFF_LR + LANES

GLA_CHUNK = 64
GLA_LEVELS = (32, 16, 8)
GLA_DIAG = 8

_TM_IN = 512
_R_GLA = 512
_TQ = 1024
_TM_OUT = 512
_PACK_ROWS = 256
_DIFF_LANE_WINDOW = 256
_DIFF_PAIRS_PER_TRIP = 4
_VMEM_LIMIT = 60 * 1024 * 1024


def _lambda_init(layer):
    return 0.8 - 0.6 * math.exp(-0.3 * layer)


def _silu(g):
    half = 0.5 * g
    return half + half * jnp.tanh(half)


def _split_hi_lo(a):
    hi = a.astype(_ACT)
    lo = (a - hi.astype(F32)).astype(_ACT)
    return hi, lo


def _pack_kernel(w_ref, o_ref):
    lr0 = W_OFF_DIFF
    wide = W_OFF_LR - W_OFF_DIFF
    o_ref[:, 0:lr0] = w_ref[:, 0:lr0].astype(_ACT)
    o_ref[:, W_OFF_DIFF:W_OFF_LR] = w_ref[:, lr0 + GLA_LOWRANK:lr0 + GLA_LOWRANK + wide].astype(_ACT)
    lane = lax.broadcasted_iota(jnp.int32, (w_ref.shape[0], LANES), 1)
    o_ref[:, W_OFF_LR:W_COLS] = jnp.where(lane < GLA_LOWRANK, w_ref[:, lr0:lr0 + LANES], 0.0).astype(_ACT)


def _pack_w_in(w):
    D, n = w.shape
    assert n == W_COLS - LANES + GLA_LOWRANK
    rb = _tile(_PACK_ROWS, D)
    return pl.pallas_call(
        _pack_kernel,
        grid=(D // rb,),
        in_specs=[pl.BlockSpec((rb, n), lambda i: (i, 0))],
        out_specs=pl.BlockSpec((rb, W_COLS), lambda i: (i, 0)),
        out_shape=jax.ShapeDtypeStruct((D, W_COLS), _ACT),
        compiler_params=pltpu.CompilerParams(vmem_limit_bytes=_VMEM_LIMIT),
        name="packw",
    )(w)


def _memkv_kernel(mem_ref, w_ref, mkT_ref, mv_ref):
    kv = jnp.dot(mem_ref[...].astype(_ACT), w_ref[...], preferred_element_type=F32)
    mkT_ref[...] = kv[:, :MEM_W].T.astype(_ACT)
    mv_ref[...] = kv[:, MEM_W:].astype(_ACT)


def _memkv(mem2d, w_kv, B, M):
    D = mem2d.shape[1]
    return pl.pallas_call(
        _memkv_kernel,
        grid=(B,),
        in_specs=[pl.BlockSpec((M, D), lambda b: (b, 0)),
                  pl.BlockSpec((D, 2 * MEM_W), lambda b: (0, 0))],
        out_specs=[pl.BlockSpec((None, MEM_W, M), lambda b: (b, 0, 0)),
                   pl.BlockSpec((M, MEM_W), lambda b: (b, 0))],
        out_shape=[jax.ShapeDtypeStruct((B, MEM_W, M), _ACT),
                   jax.ShapeDtypeStruct((B * M, MEM_W), _ACT)],
        compiler_params=pltpu.CompilerParams(vmem_limit_bytes=_VMEM_LIMIT),
        name="memkv",
    )(mem2d, w_kv)


def _inproj_kernel(x_ref, w_ref, wgk_ref, bgk_ref, cos_ref, sin_ref, mkT_ref, mv_ref,
                   gq_ref, gk_ref, gv_ref, gg_ref, la_ref, dqT_ref, dk_ref, dvT_ref, dg_ref, xat_ref):
    tm = x_ref.shape[0]
    xb = x_ref[...].astype(_ACT)

    def proj(base, c0, c1):
        return jnp.dot(xb, w_ref[:, base + c0:base + c1], preferred_element_type=F32)

    gq_ref[...] = (proj(W_OFF_GLA,0, GLA_KW) * (GLA_DK ** -0.5)).astype(_ACT)
    gk_ref[...] = proj(W_OFF_GLA,GLA_KW, 2 * GLA_KW).astype(_ACT)
    gv_ref[...] = proj(W_OFF_GLA,2 * GLA_KW, 2 * GLA_KW + GLA_W).astype(_ACT)
    gg_ref[...] = proj(W_OFF_GLA,2 * GLA_KW + GLA_W, 2 * GLA_KW + 2 * GLA_W).astype(_ACT)

    glr = proj(W_OFF_LR,0, LANES)
    hi, lo = _split_hi_lo(glr)
    logit = jnp.dot(jnp.concatenate([hi, hi, lo], axis=1), wgk_ref[...],
                    preferred_element_type=F32) + bgk_ref[...]
    log_sig = jnp.minimum(logit, 0.0) - jnp.log(1.0 + jnp.exp(-jnp.abs(logit)))
    la_ref[...] = log_sig * (LOG2E / GLA_TAU)

    lane = lax.broadcasted_iota(jnp.int32, (tm, LANES), 1) & (DIFF_DQK - 1)
    cosv = cos_ref[...]
    sinv = sin_ref[...]
    half = ROT_DIM // 2

    def rope(a):
        up = pltpu.roll(a, LANES - half, 1)
        dn = pltpu.roll(a, half, 1)
        return a * cosv + jnp.where(lane < half, -up, dn) * sinv

    dq = proj(W_OFF_DIFF,0, DIFF_W)
    dk = proj(W_OFF_DIFF,DIFF_W, 2 * DIFF_W)
    dv = proj(W_OFF_DIFF,2 * DIFF_W, 3 * DIFF_W)
    ones = jnp.ones((ACT_SUBLANES, tm), _ACT)
    for h in range(DIFF_HEADS):
        sl = slice(h * DIFF_DV, (h + 1) * DIFF_DV)
        dqT_ref[sl, :] = (rope(dq[:, sl]) * (DIFF_DQK ** -0.5 * LOG2E)).T.astype(_ACT)
        dk_ref[:, sl] = rope(dk[:, sl]).astype(_ACT)
        dvT_ref[h * DIFF_VT_ROWS:h * DIFF_VT_ROWS + DIFF_DV, :] = dv[:, sl].T.astype(_ACT)
        dvT_ref[h * DIFF_VT_ROWS + DIFF_DV:(h + 1) * DIFF_VT_ROWS, :] = ones
    dg_ref[...] = proj(W_OFF_DIFF,3 * DIFF_W, 4 * DIFF_W).astype(_ACT)

    mq = proj(W_OFF_MEM,0, MEM_W)
    mg = proj(W_OFF_MEM,MEM_W, 2 * MEM_W)
    for h in range(MEM_HEADS):
        sl = slice(h * MEM_DH, (h + 1) * MEM_DH)
        qh = (mq[:, sl] * (MEM_DH ** -0.5)).astype(_ACT)
        s = jnp.dot(qh, mkT_ref[sl, :], preferred_element_type=F32)
        p = jnp.exp(s - jnp.max(s, axis=-1, keepdims=True))
        l = jnp.sum(p, axis=-1, keepdims=True)
        o = jnp.dot(p.astype(_ACT), mv_ref[:, sl], preferred_element_type=F32) / l
        xat_ref[:, sl] = (o * _silu(mg[:, sl])).astype(_ACT)


def _inproj(x2d, w_all, wgk3, bgk, cos_t, sin_t, mkT, mv, B, S, M, tm):
    T, D = x2d.shape
    ns = S // tm
    row = lambda b, i: (b * ns + i, 0)
    const = lambda b, i: (0, 0)
    whole = lambda a: pl.BlockSpec(a.shape, const, pipeline_mode=pl.Buffered(1))
    rows = lambda w: pl.BlockSpec((tm, w), row)
    tr = pl.BlockSpec((None, DIFF_W, tm), lambda b, i: (b, 0, i))
    trv = pl.BlockSpec((None, DIFF_HEADS * DIFF_VT_ROWS, tm), lambda b, i: (b, 0, i))
    return pl.pallas_call(
        _inproj_kernel,
        grid=(B, ns),
        in_specs=[rows(D), whole(w_all), whole(wgk3), whole(bgk),
                  rows(LANES), rows(LANES),
                  pl.BlockSpec((None, MEM_W, M), lambda b, i: (b, 0, 0)),
                  pl.BlockSpec((M, MEM_W), lambda b, i: (b, 0))],
        out_specs=[rows(GLA_KW), rows(GLA_KW), rows(GLA_W), rows(GLA_W), rows(GLA_KW),
                   tr, rows(DIFF_W), trv, rows(DIFF_W), rows(MEM_W)],
        out_shape=[jax.ShapeDtypeStruct((T, GLA_KW), _ACT), jax.ShapeDtypeStruct((T, GLA_KW), _ACT),
                   jax.ShapeDtypeStruct((T, GLA_W), _ACT), jax.ShapeDtypeStruct((T, GLA_W), _ACT),
                   jax.ShapeDtypeStruct((T, GLA_KW), F32),
                   jax.ShapeDtypeStruct((B, DIFF_W, S), _ACT), jax.ShapeDtypeStruct((T, DIFF_W), _ACT),
                   jax.ShapeDtypeStruct((B, DIFF_HEADS * DIFF_VT_ROWS, S), _ACT),
                   jax.ShapeDtypeStruct((T, DIFF_W), _ACT),
                   jax.ShapeDtypeStruct((T, MEM_W), _ACT)],
        compiler_params=pltpu.CompilerParams(
            dimension_semantics=("arbitrary", "arbitrary"), vmem_limit_bytes=_VMEM_LIMIT),
        name="inproj",
    )(x2d, w_all, wgk3, bgk, cos_t, sin_t, mkT, mv)


def _gla_constants():
    C = GLA_CHUNK
    t = np.arange(C)
    tril = (t[None, :] <= t[:, None]).astype(np.float32)
    rsel = np.zeros((GLA_DIAG * GLA_DK, C), np.float32)
    for j in range(GLA_DIAG):
        rsel[j * GLA_DK:(j + 1) * GLA_DK, j::GLA_DIAG] = 1.0
    return tril, rsel


def _gla_kernel(q_ref, k_ref, v_ref, g_ref, la_ref, tril_ref, rsel_ref, gng_ref, o_ref, st_ref):
    R = q_ref.shape[0]
    C = GLA_CHUNK

    @pl.when(pl.program_id(1) == 0)
    def _():
        st_ref[...] = jnp.zeros_like(st_ref)

    ri = lax.broadcasted_iota(jnp.int32, (C, C), 0)
    ci = lax.broadcasted_iota(jnp.int32, (C, C), 1)
    rk = lax.broadcasted_iota(jnp.int32, (C, GLA_KW), 0)
    pair_mask = [((ri // (2 * s)) == (ci // (2 * s))) & (((ri // s) & 1) == 1) & (((ci // s) & 1) == 0)
                 for s in GLA_LEVELS]
    half_sign = [jnp.where(((rk // s) & 1) == 1, 1.0, -1.0) for s in GLA_LEVELS]
    diag_mask = ((ri // GLA_DIAG) == (ci // GLA_DIAG)) & (ci <= ri)
    groups = C // GLA_DIAG
    tril = tril_ref[...]
    rsel = rsel_ref[...]
    gng = gng_ref[...]

    heads = range(GLA_HEADS)
    sk = [slice(h * GLA_DK, (h + 1) * GLA_DK) for h in heads]
    sv = [slice(h * GLA_DV, (h + 1) * GLA_DV) for h in heads]
    nt = (((1,), (1,)), ((), ()))

    def group_bcast(a, j):
        a3 = a.reshape(groups, GLA_DIAG, GLA_KW)
        return jnp.broadcast_to(a3[:, j:j + 1, :], (groups, GLA_DIAG, GLA_KW)).reshape(C, GLA_KW)

    def chunk(c, carry):
        rows = pl.ds(pl.multiple_of(c * C, C), C)
        q = q_ref[rows, :].astype(F32)
        k = k_ref[rows, :].astype(F32)
        hi, lo = _split_hi_lo(la_ref[rows, :])
        cs = jnp.dot(tril, jnp.concatenate([hi, lo], axis=1), preferred_element_type=F32)
        b = cs[:, :GLA_KW] + cs[:, GLA_KW:]

        a = [jnp.zeros((C, C), F32) for _ in heads]
        for li, s in enumerate(GLA_LEVELS):
            ref = jnp.concatenate(
                [jnp.broadcast_to(b[p + s - 1:p + s, :], (2 * s, GLA_KW)) for p in range(0, C, 2 * s)],
                axis=0)
            x = jnp.exp2(half_sign[li] * (b - ref))
            qx = (q * x).astype(_ACT)
            kx = (k * x).astype(_ACT)
            for h in heads:
                al = lax.dot_general(qx[:, sk[h]], kx[:, sk[h]], nt, preferred_element_type=F32)
                a[h] = jnp.where(pair_mask[li], al, a[h])
        ps = []
        for j in range(GLA_DIAG):
            dec = jnp.exp2(jnp.minimum(b - group_bcast(b, j), 0.0))
            ps.append((q * group_bcast(k, j) * dec).astype(_ACT))
        for h in heads:
            ad = jnp.dot(jnp.concatenate([p[:, sk[h]] for p in ps], axis=1), rsel,
                         preferred_element_type=F32)
            a[h] = jnp.where(diag_mask, ad, a[h])

        qe = (q * jnp.exp2(b)).astype(_ACT)
        b_last = b[C - 1:C, :]
        kd = (k * jnp.exp2(b_last - b)).astype(_ACT)
        keep = jnp.exp2(b_last)
        for h in heads:
            v = v_ref[rows, sv[h]]
            st = st_ref[h]
            o = lax.dot_general(qe[:, sk[h]], st.astype(_ACT), nt, preferred_element_type=F32)
            o = o + jnp.dot(a[h].astype(_ACT), v, preferred_element_type=F32)
            upd = lax.dot_general(v, kd[:, sk[h]], (((0,), (0,)), ((), ())), preferred_element_type=F32)
            st_ref[h] = st * keep[:, sk[h]] + upd

            ms = jnp.mean(o * o, axis=-1, keepdims=True)
            y = o * lax.rsqrt(ms + GLA_NORM_EPS) * gng
            o_ref[rows, sv[h]] = (y * _silu(g_ref[rows, sv[h]].astype(F32))).astype(_ACT)
        return carry

    lax.fori_loop(0, R // C, chunk, 0, unroll=8)


def _gla(gq, gk, gv, gg, la, gng, B, S, R):
    T = gq.shape[0]
    ns = S // R
    tril, rsel = _gla_constants()
    tril = jnp.asarray(tril, _ACT)
    rsel = jnp.asarray(rsel, _ACT)
    row = lambda b, i: (b * ns + i, 0)
    const = lambda b, i: (0, 0)
    rows = lambda w: pl.BlockSpec((R, w), row)
    whole = lambda a: pl.BlockSpec(a.shape, const)
    return pl.pallas_call(
        _gla_kernel,
        grid=(B, ns),
        in_specs=[rows(GLA_KW), rows(GLA_KW), rows(GLA_W), rows(GLA_W), rows(GLA_KW),
                  whole(tril), whole(rsel), whole(gng)],
        out_specs=rows(GLA_W),
        out_shape=jax.ShapeDtypeStruct((T, GLA_W), _ACT),
        scratch_shapes=[pltpu.VMEM((GLA_HEADS, GLA_DV, GLA_DK), F32)],
        compiler_params=pltpu.CompilerParams(
            dimension_semantics=("arbitrary", "arbitrary"), vmem_limit_bytes=_VMEM_LIMIT),
        name="gla",
    )(gq, gk, gv, gg, la, tril, rsel, gng)


def _diff_kernel(qT_ref, qTn_ref, k_ref, vT_ref, g_ref, lq1_ref, lk1_ref, lq2_ref, lk2_ref, gcol_ref, o_ref,
                 s0_ref, s1_ref, bm0_ref, bm1_ref, m_ref, acc_ref, *, lam_init):
    s_refs, bm_refs = (s0_ref, s1_ref), (bm0_ref, bm1_ref)
    tq = qT_ref.shape[1]
    tk = s0_ref.shape[1]
    assert tq == 2 * tk
    qi = pl.program_id(2)
    nfull = 2 * qi
    comp_row = lax.broadcasted_iota(jnp.int32, qT_ref.shape, 0) < DIFF_DQK

    def components(q):
        zero = jnp.zeros_like(q)
        return jnp.where(comp_row, q, zero), jnp.where(comp_row, zero, q)

    qcs = components(qT_ref[...])
    acc_ref[...] = jnp.zeros_like(acc_ref)
    m_ref[...] = jnp.full_like(m_ref, -jnp.inf)

    def produce(ki, slot, c, w, lw=tk, q=None):
        ln = slice(w * lw, (w + 1) * lw)
        kb = k_ref[pl.ds(pl.multiple_of(ki * tk, tk), tk), :]
        s = jnp.dot(kb, (qcs if q is None else q)[c][:, ln], preferred_element_type=F32)
        s_refs[slot][c, :, ln] = s
        bm_refs[slot][c, :, ln] = jnp.max(s, axis=0, keepdims=True)

    def consume(ki, slot, c, w, masked=False, lw=tk):
        ln = slice(w * lw, (w + 1) * lw)
        vb = vT_ref[:, pl.ds(pl.multiple_of(ki * tk, tk), tk)]
        s = s_refs[slot][c, :, ln]
        if masked:
            key = lax.broadcasted_iota(jnp.int32, s.shape, 0)
            qry = lax.broadcasted_iota(jnp.int32, s.shape, 1)
            s = jnp.where(key <= qry, s, -jnp.inf)
            bm = jnp.max(s, axis=0, keepdims=True)
        else:
            bm = bm_refs[slot][c, :, ln]
        m_old = m_ref[c, :, ln]
        m_new = jnp.maximum(m_old, bm)
        p = jnp.exp2(s - m_new).astype(_ACT)
        acc_ref[c, :, ln] = (jnp.exp2(m_old - m_new) * acc_ref[c, :, ln]
                             + jnp.dot(vb, p, preferred_element_type=F32))
        m_ref[c, :, ln] = m_new

    lws = min(_DIFF_LANE_WINDOW, tk)

    def step(k_next, slot_next, k_cur, slot_cur):
        for c in range(2):
            for w in range(tq // lws):
                produce(k_next, slot_next, c, w, lws)
                consume(k_cur, slot_cur, c, w, lw=lws)

    @pl.when(qi == 0)
    def _():
        for c in range(2):
            for w in range(2):
                produce(0, 0, c, w)

    def pair(j):
        step(2 * j + 1, 1, 2 * j, 0)
        step(2 * j + 2, 0, 2 * j + 1, 1)

    def pairs(t, carry):
        for u in range(_DIFF_PAIRS_PER_TRIP):
            pair(_DIFF_PAIRS_PER_TRIP * t + u)
        return carry

    assert _DIFF_PAIRS_PER_TRIP == 4
    full = qi // 4
    lax.fori_loop(0, full, pairs, 0)
    rem = qi - 4 * full

    @pl.when(rem >= 2)
    def _():
        pair(4 * full)
        pair(4 * full + 1)

    @pl.when(rem % 2 == 1)
    def _():
        pair(qi - 1)

    qn = components(qTn_ref[...])
    produce(nfull + 1, 1, 0, 1)
    consume(nfull, 0, 0, 0, masked=True)
    consume(nfull, 0, 0, 1)
    produce(nfull + 1, 1, 1, 1)
    consume(nfull, 0, 1, 0, masked=True)
    produce(0, 0, 0, 0, q=qn)
    consume(nfull, 0, 1, 1)
    produce(0, 0, 0, 1, q=qn)
    consume(nfull + 1, 1, 0, 1, masked=True)
    produce(0, 0, 1, 0, q=qn)
    consume(nfull + 1, 1, 1, 1, masked=True)
    produce(0, 0, 1, 1, q=qn)

    lam = (jnp.exp(jnp.sum(lq1_ref[...] * lk1_ref[...], axis=-1, keepdims=True))
           - jnp.exp(jnp.sum(lq2_ref[...] * lk2_ref[...], axis=-1, keepdims=True)) + lam_init)
    l1 = acc_ref[0, DIFF_DV:DIFF_DV + 1, :]
    l2 = acc_ref[1, DIFF_DV:DIFF_DV + 1, :]
    oT = acc_ref[0, 0:DIFF_DV, :] / l1 - lam * (acc_ref[1, 0:DIFF_DV, :] / l2)
    ms = jnp.mean(oT * oT, axis=0, keepdims=True)
    yT = oT * lax.rsqrt(ms + DIFF_NORM_EPS) * (gcol_ref[...] * (1.0 - lam_init))
    o_ref[...] = (yT.T * _silu(g_ref[...].astype(F32))).astype(_ACT)


def _diff(dqT, dk, dvT, dg, lq1, lk1, lq2, lk2, gcol, B, S, tq, lam_init):
    T = dk.shape[0]
    nq = S // tq
    small = lambda a: pl.BlockSpec(a.shape, lambda b, h, i: (0, 0))
    blk = pl.BlockSpec((tq, DIFF_DV), lambda b, h, i: (b * nq + i, h))
    return pl.pallas_call(
        functools.partial(_diff_kernel, lam_init=lam_init),
        grid=(B, DIFF_HEADS, nq),
        in_specs=[pl.BlockSpec((None, DIFF_DV, tq), lambda b, h, i: (b, h, i)),
                  pl.BlockSpec((None, DIFF_DV, tq), lambda b, h, i: (b, h, jnp.minimum(i + 1, nq - 1))),
                  pl.BlockSpec((S, DIFF_DV), lambda b, h, i: (b, h)),
                  pl.BlockSpec((None, DIFF_VT_ROWS, S), lambda b, h, i: (b, h, 0)),
                  blk, small(lq1), small(lk1), small(lq2), small(lk2), small(gcol)],
        out_specs=blk,
        out_shape=jax.ShapeDtypeStruct((T, DIFF_W), _ACT),
        scratch_shapes=[pltpu.VMEM((2, tq // 2, tq), F32), pltpu.VMEM((2, tq // 2, tq), F32),
                        pltpu.VMEM((2, 1, tq), F32), pltpu.VMEM((2, 1, tq), F32),
                        pltpu.VMEM((2, 1, tq), F32), pltpu.VMEM((2, DIFF_VT_ROWS, tq), F32)],
        compiler_params=pltpu.CompilerParams(
            dimension_semantics=("arbitrary", "arbitrary", "arbitrary"), vmem_limit_bytes=_VMEM_LIMIT),
        name="diffattn",
    )(dqT, dqT, dk, dvT, dg, lq1, lk1, lq2, lk2, gcol)


def _outproj_kernel(gla_ref, dif_ref, xat_ref, x_ref, w_ref, g_ref, b_ref, o_ref, *, alpha):
    y = alpha * x_ref[...]
    y = y + jnp.dot(gla_ref[...], w_ref[0:GLA_W, :], preferred_element_type=F32)
    y = y + jnp.dot(dif_ref[...], w_ref[GLA_W:GLA_W + DIFF_W, :], preferred_element_type=F32)
    y = y + jnp.dot(xat_ref[...], w_ref[GLA_W + DIFF_W:GLA_W + DIFF_W + MEM_W, :], preferred_element_type=F32)
    mu = jnp.mean(y, axis=-1, keepdims=True)
    yc = y - mu
    var = jnp.mean(yc * yc, axis=-1, keepdims=True)
    o_ref[...] = yc * lax.rsqrt(var + LN_EPS) * g_ref[...] + b_ref[...]


def _outproj(gla, dif, xat, x2d, w_out, ln_g, ln_b, tm, alpha):
    T, D = x2d.shape
    row = lambda i: (i, 0)
    const = lambda i: (0, 0)
    rows = lambda w: pl.BlockSpec((tm, w), row)
    return pl.pallas_call(
        functools.partial(_outproj_kernel, alpha=alpha),
        grid=(T // tm,),
        in_specs=[rows(GLA_W), rows(DIFF_W), rows(MEM_W), rows(D),
                  pl.BlockSpec(w_out.shape, const, pipeline_mode=pl.Buffered(1)),
                  pl.BlockSpec(ln_g.shape, const), pl.BlockSpec(ln_b.shape, const)],
        out_specs=rows(D),
        out_shape=jax.ShapeDtypeStruct((T, D), F32),
        compiler_params=pltpu.CompilerParams(
            dimension_semantics=("arbitrary",), vmem_limit_bytes=_VMEM_LIMIT),
        name="outproj",
    )(gla, dif, xat, x2d, w_out, ln_g, ln_b)


def _rotary_tables(positions):
    B, S = positions.shape
    half = ROT_DIM // 2
    inv_freq = jnp.power(jnp.float32(ROPE_THETA), -(jnp.arange(0, ROT_DIM, 2, dtype=F32) / ROT_DIM))
    lane = np.arange(LANES) % DIFF_DQK
    lane_freq = jnp.where(lane < ROT_DIM, inv_freq[lane % half], 0.0)
    ang = positions.astype(F32).reshape(B * S, 1) * lane_freq[None, :]
    return jnp.cos(ang), jnp.sin(ang)


def _tile(default, n):
    t = min(default, n)
    assert n % t == 0, (n, t)
    return t


def kernel(x, mem, positions, w_in, w_gk_up, b_gk_up, gla_norm_g, lambda_q1, lambda_k1, lambda_q2,
           lambda_k2, diff_norm_g, w_mem_kv, w_out, ln_g, ln_b):
    B, S, D = x.shape
    M = mem.shape[1]
    depth = w_in.shape[0]
    T = B * S
    assert S % GLA_CHUNK == 0 and M % LANES == 0
    tm_in, r_gla, tq, tm_out = _tile(_TM_IN, S), _tile(_R_GLA, S), _tile(_TQ, S), _tile(_TM_OUT, S)
    alpha = (2.0 * depth) ** 0.25

    cos_t, sin_t = _rotary_tables(positions)
    mem2d = mem.reshape(B * M, D)
    h = x.reshape(T, D)

    o_lr = 2 * GLA_KW + 2 * GLA_W
    o_d = o_lr + GLA_LOWRANK
    o_m = o_d + 4 * DIFF_W
    for l in range(depth):
        w_all = _pack_w_in(w_in[l])
        wgk = jnp.pad(w_gk_up[l].astype(F32), ((0, LANES - GLA_LOWRANK), (0, 0)))
        wgk_hi, wgk_lo = _split_hi_lo(wgk)
        wgk3 = jnp.concatenate([wgk_hi, wgk_lo, wgk_hi], axis=0)
        bgk = b_gk_up[l].astype(F32).reshape(1, GLA_KW)

        mkT, mv = _memkv(mem2d, w_mem_kv[l].astype(_ACT), B, M)
        gq, gk, gv, gg, la, dqT, dk, dvT, dg, xat = _inproj(
            h, w_all, wgk3, bgk, cos_t, sin_t, mkT, mv, B, S, M, tm_in)
        gla = _gla(gq, gk, gv, gg, la, gla_norm_g[l].astype(F32).reshape(1, GLA_DV), B, S, r_gla)
        row64 = lambda a: a[l].astype(F32).reshape(1, DIFF_DQK)
        dif = _diff(dqT, dk, dvT, dg, row64(lambda_q1), row64(lambda_k1), row64(lambda_q2),
                    row64(lambda_k2), diff_norm_g[l].astype(F32).reshape(DIFF_DV, 1), B, S, tq,
                    _lambda_init(l))
        h = _outproj(gla, dif, xat, h, w_out[l].astype(_ACT), ln_g[l].astype(F32).reshape(1, D),
                     ln_b[l].astype(F32).reshape(1, D), tm_out, alpha)
    return h.reshape(B, S, D)
```

```python
import functools
import math

import jax
import jax.numpy as jnp
import numpy as np
from jax import lax
from jax.experimental import pallas as pl
from jax.experimental.pallas import tpu as pltpu

F32 = jnp.float32
_ACT = jnp.bfloat16

GLA_HEADS = 4
GLA_DK = 128
GLA_DV = 256
GLA_LOWRANK = 16
GLA_TAU = 16.0
GLA_NORM_EPS = 1e-6
GLA_W = GLA_HEADS * GLA_DV
GLA_KW = GLA_HEADS * GLA_DK

DIFF_HEADS = 4
DIFF_DV = 128
DIFF_DQK = 64
DIFF_NORM_EPS = 1e-5
DIFF_W = DIFF_HEADS * DIFF_DV
ACT_SUBLANES = 16
DIFF_VT_ROWS = DIFF_DV + ACT_SUBLANES
LOG2E = math.log2(math.e)

MEM_HEADS = 4
MEM_DH = 128
MEM_W = MEM_HEADS * MEM_DH

ROPE_THETA = 500000.0
ROT_DIM = DIFF_DQK // 4
LN_EPS = 1e-5

LANES = 128
SUBLANES = 8

W_OFF_GLA = 0
W_OFF_DIFF = W_OFF_GLA + 2 * GLA_KW + 2 * GLA_W
W_OFF_MEM = W_OFF_DIFF + 4 * DIFF_W
W_OFF_LR = W_OFF_MEM + 2 * MEM_W
W_COLS = W_OFF_LR + LANES

GLA_CHUNK = 64
GLA_LEVELS = (32, 16, 8)
GLA_DIAG = 8

_TM_IN = 512
_R_GLA = 512
_TQ = 1024
_TM_OUT = 512
_PACK_ROWS = 256
_DIFF_LANE_WINDOW = 256
_DIFF_PAIRS_PER_TRIP = 4
_VMEM_LIMIT = 60 * 1024 * 1024


def _lambda_init(layer):
    return 0.8 - 0.6 * math.exp(-0.3 * layer)


def _silu(g):
    half = 0.5 * g
    return half + half * jnp.tanh(half)


def _split_hi_lo(a):
    hi = a.astype(_ACT)
    lo = (a - hi.astype(F32)).astype(_ACT)
    return hi, lo


def _pack_kernel(wT_ref, o_ref, p_ref):
    lr0 = W_OFF_DIFF
    wide = W_OFF_LR - W_OFF_DIFF
    p_ref[0:lr0, :] = wT_ref[0:lr0, :]
    p_ref[W_OFF_DIFF:W_OFF_LR, :] = wT_ref[lr0 + GLA_LOWRANK:lr0 + GLA_LOWRANK + wide, :]
    p_ref[W_OFF_LR:W_OFF_LR + GLA_LOWRANK, :] = wT_ref[lr0:lr0 + GLA_LOWRANK, :]
    p_ref[W_OFF_LR + GLA_LOWRANK:W_COLS, :] = jnp.zeros((LANES - GLA_LOWRANK, p_ref.shape[1]), F32)
    o_ref[...] = p_ref[...].T.astype(_ACT)


def _pack_w_in(w_in, layer):
    _, D, n = w_in.shape
    assert n == W_COLS - LANES + GLA_LOWRANK
    rb = _tile(_PACK_ROWS, D)
    return pl.pallas_call(
        _pack_kernel,
        grid=(D // rb,),
        in_specs=[pl.BlockSpec((None, n, rb), lambda i: (layer, 0, i))],
        out_specs=pl.BlockSpec((rb, W_COLS), lambda i: (i, 0)),
        out_shape=jax.ShapeDtypeStruct((D, W_COLS), _ACT),
        scratch_shapes=[pltpu.VMEM((W_COLS, rb), F32)],
        compiler_params=pltpu.CompilerParams(vmem_limit_bytes=_VMEM_LIMIT),
        name="packw",
    )(jnp.swapaxes(w_in, 1, 2))


def _memkv_kernel(mem_ref, w_ref, mkT_ref, mv_ref):
    kv = jnp.dot(mem_ref[...].astype(_ACT), w_ref[...], preferred_element_type=F32)
    mkT_ref[...] = kv[:, :MEM_W].T.astype(_ACT)
    mv_ref[...] = kv[:, MEM_W:].astype(_ACT)


def _memkv(mem2d, w_kv, B, M):
    D = mem2d.shape[1]
    return pl.pallas_call(
        _memkv_kernel,
        grid=(B,),
        in_specs=[pl.BlockSpec((M, D), lambda b: (b, 0)),
                  pl.BlockSpec((D, 2 * MEM_W), lambda b: (0, 0))],
        out_specs=[pl.BlockSpec((None, MEM_W, M), lambda b: (b, 0, 0)),
                   pl.BlockSpec((M, MEM_W), lambda b: (b, 0))],
        out_shape=[jax.ShapeDtypeStruct((B, MEM_W, M), _ACT),
                   jax.ShapeDtypeStruct((B * M, MEM_W), _ACT)],
        compiler_params=pltpu.CompilerParams(vmem_limit_bytes=_VMEM_LIMIT),
        name="memkv",
    )(mem2d, w_kv)


def _inproj_kernel(x_ref, w_ref, wgk_ref, bgk_ref, cos_ref, sin_ref, mkT_ref, mv_ref,
                   gq_ref, gk_ref, gv_ref, gg_ref, la_ref, dqT_ref, dk_ref, dvT_ref, dg_ref, xat_ref):
    tm = x_ref.shape[0]
    xb = x_ref[...].astype(_ACT)

    def proj(base, c0, c1):
        return jnp.dot(xb, w_ref[:, base + c0:base + c1], preferred_element_type=F32)

    gq_ref[...] = (proj(W_OFF_GLA,0, GLA_KW) * (GLA_DK ** -0.5)).astype(_ACT)
    gk_ref[...] = proj(W_OFF_GLA,GLA_KW, 2 * GLA_KW).astype(_ACT)
    gv_ref[...] = proj(W_OFF_GLA,2 * GLA_KW, 2 * GLA_KW + GLA_W).astype(_ACT)
    gg_ref[...] = proj(W_OFF_GLA,2 * GLA_KW + GLA_W, 2 * GLA_KW + 2 * GLA_W).astype(_ACT)

    glr = proj(W_OFF_LR,0, LANES)
    hi, lo = _split_hi_lo(glr)
    logit = jnp.dot(jnp.concatenate([hi, hi, lo], axis=1), wgk_ref[...],
                    preferred_element_type=F32) + bgk_ref[...]
    log_sig = jnp.minimum(logit, 0.0) - jnp.log(1.0 + jnp.exp(-jnp.abs(logit)))
    la_ref[...] = log_sig * (LOG2E / GLA_TAU)

    lane = lax.broadcasted_iota(jnp.int32, (tm, LANES), 1) & (DIFF_DQK - 1)
    cosv = cos_ref[...]
    sinv = sin_ref[...]
    half = ROT_DIM // 2

    def rope(a):
        up = pltpu.roll(a, LANES - half, 1)
        dn = pltpu.roll(a, half, 1)
        return a * cosv + jnp.where(lane < half, -up, dn) * sinv

    dq = proj(W_OFF_DIFF,0, DIFF_W)
    dk = proj(W_OFF_DIFF,DIFF_W, 2 * DIFF_W)
    dv = proj(W_OFF_DIFF,2 * DIFF_W, 3 * DIFF_W)
    ones = jnp.ones((ACT_SUBLANES, tm), _ACT)
    for h in range(DIFF_HEADS):
        sl = slice(h * DIFF_DV, (h + 1) * DIFF_DV)
        dqT_ref[sl, :] = (rope(dq[:, sl]) * (DIFF_DQK ** -0.5 * LOG2E)).T.astype(_ACT)
        dk_ref[:, sl] = rope(dk[:, sl]).astype(_ACT)
        dvT_ref[h * DIFF_VT_ROWS:h * DIFF_VT_ROWS + DIFF_DV, :] = dv[:, sl].T.astype(_ACT)
        dvT_ref[h * DIFF_VT_ROWS + DIFF_DV:(h + 1) * DIFF_VT_ROWS, :] = ones
    dg_ref[...] = proj(W_OFF_DIFF,3 * DIFF_W, 4 * DIFF_W).astype(_ACT)

    mq = proj(W_OFF_MEM,0, MEM_W)
    mg = proj(W_OFF_MEM,MEM_W, 2 * MEM_W)
    for h in range(MEM_HEADS):
        sl = slice(h * MEM_DH, (h + 1) * MEM_DH)
        qh = (mq[:, sl] * (MEM_DH ** -0.5)).astype(_ACT)
        s = jnp.dot(qh, mkT_ref[sl, :], preferred_element_type=F32)
        p = jnp.exp(s - jnp.max(s, axis=-1, keepdims=True))
        l = jnp.sum(p, axis=-1, keepdims=True)
        o = jnp.dot(p.astype(_ACT), mv_ref[:, sl], preferred_element_type=F32) / l
        xat_ref[:, sl] = (o * _silu(mg[:, sl])).astype(_ACT)


def _inproj(x2d, w_all, wgk3, bgk, cos_t, sin_t, mkT, mv, B, S, M, tm):
    T, D = x2d.shape
    ns = S // tm
    row = lambda b, i: (b * ns + i, 0)
    const = lambda b, i: (0, 0)
    whole = lambda a: pl.BlockSpec(a.shape, const, pipeline_mode=pl.Buffered(1))
    rows = lambda w: pl.BlockSpec((tm, w), row)
    tr = pl.BlockSpec((None, DIFF_W, tm), lambda b, i: (b, 0, i))
    trv = pl.BlockSpec((None, DIFF_HEADS * DIFF_VT_ROWS, tm), lambda b, i: (b, 0, i))
    return pl.pallas_call(
        _inproj_kernel,
        grid=(B, ns),
        in_specs=[rows(D), whole(w_all), whole(wgk3), whole(bgk),
                  rows(LANES), rows(LANES),
                  pl.BlockSpec((None, MEM_W, M), lambda b, i: (b, 0, 0)),
                  pl.BlockSpec((M, MEM_W), lambda b, i: (b, 0))],
        out_specs=[rows(GLA_KW), rows(GLA_KW), rows(GLA_W), rows(GLA_W), rows(GLA_KW),
                   tr, rows(DIFF_W), trv, rows(DIFF_W), rows(MEM_W)],
        out_shape=[jax.ShapeDtypeStruct((T, GLA_KW), _ACT), jax.ShapeDtypeStruct((T, GLA_KW), _ACT),
                   jax.ShapeDtypeStruct((T, GLA_W), _ACT), jax.ShapeDtypeStruct((T, GLA_W), _ACT),
                   jax.ShapeDtypeStruct((T, GLA_KW), F32),
                   jax.ShapeDtypeStruct((B, DIFF_W, S), _ACT), jax.ShapeDtypeStruct((T, DIFF_W), _ACT),
                   jax.ShapeDtypeStruct((B, DIFF_HEADS * DIFF_VT_ROWS, S), _ACT),
                   jax.ShapeDtypeStruct((T, DIFF_W), _ACT),
                   jax.ShapeDtypeStruct((T, MEM_W), _ACT)],
        compiler_params=pltpu.CompilerParams(
            dimension_semantics=("arbitrary", "arbitrary"), vmem_limit_bytes=_VMEM_LIMIT),
        name="inproj",
    )(x2d, w_all, wgk3, bgk, cos_t, sin_t, mkT, mv)


def _gla_constants():
    C = GLA_CHUNK
    t = np.arange(C)
    tril = (t[None, :] <= t[:, None]).astype(np.float32)
    rsel = np.zeros((GLA_DIAG * GLA_DK, C), np.float32)
    for j in range(GLA_DIAG):
        rsel[j * GLA_DK:(j + 1) * GLA_DK, j::GLA_DIAG] = 1.0
    return tril, rsel


def _gla_kernel(q_ref, k_ref, v_ref, g_ref, la_ref, tril_ref, rsel_ref, gng_ref, o_ref, st_ref):
    R = q_ref.shape[0]
    C = GLA_CHUNK

    @pl.when(pl.program_id(1) == 0)
    def _():
        st_ref[...] = jnp.zeros_like(st_ref)

    ri = lax.broadcasted_iota(jnp.int32, (C, C), 0)
    ci = lax.broadcasted_iota(jnp.int32, (C, C), 1)
    rk = lax.broadcasted_iota(jnp.int32, (C, GLA_KW), 0)
    pair_mask = [((ri // (2 * s)) == (ci // (2 * s))) & (((ri // s) & 1) == 1) & (((ci // s) & 1) == 0)
                 for s in GLA_LEVELS]
    half_sign = [jnp.where(((rk // s) & 1) == 1, 1.0, -1.0) for s in GLA_LEVELS]
    diag_mask = ((ri // GLA_DIAG) == (ci // GLA_DIAG)) & (ci <= ri)
    groups = C // GLA_DIAG
    tril = tril_ref[...]
    rsel = rsel_ref[...]
    gng = gng_ref[...]

    heads = range(GLA_HEADS)
    sk = [slice(h * GLA_DK, (h + 1) * GLA_DK) for h in heads]
    sv = [slice(h * GLA_DV, (h + 1) * GLA_DV) for h in heads]
    nt = (((1,), (1,)), ((), ()))

    def group_bcast(a, j):
        a3 = a.reshape(groups, GLA_DIAG, GLA_KW)
        return jnp.broadcast_to(a3[:, j:j + 1, :], (groups, GLA_DIAG, GLA_KW)).reshape(C, GLA_KW)

    def chunk(c, carry):
        rows = pl.ds(pl.multiple_of(c * C, C), C)
        q = q_ref[rows, :].astype(F32)
        k = k_ref[rows, :].astype(F32)
        hi, lo = _split_hi_lo(la_ref[rows, :])
        cs = jnp.dot(tril, jnp.concatenate([hi, lo], axis=1), preferred_element_type=F32)
        b = cs[:, :GLA_KW] + cs[:, GLA_KW:]

        a = [jnp.zeros((C, C), F32) for _ in heads]
        for li, s in enumerate(GLA_LEVELS):
            ref = jnp.concatenate(
                [jnp.broadcast_to(b[p + s - 1:p + s, :], (2 * s, GLA_KW)) for p in range(0, C, 2 * s)],
                axis=0)
            x = jnp.exp2(half_sign[li] * (b - ref))
            qx = (q * x).astype(_ACT)
            kx = (k * x).astype(_ACT)
            for h in heads:
                al = lax.dot_general(qx[:, sk[h]], kx[:, sk[h]], nt, preferred_element_type=F32)
                a[h] = jnp.where(pair_mask[li], al, a[h])
        ps = []
        for j in range(GLA_DIAG):
            dec = jnp.exp2(jnp.minimum(b - group_bcast(b, j), 0.0))
            ps.append((q * group_bcast(k, j) * dec).astype(_ACT))
        for h in heads:
            ad = jnp.dot(jnp.concatenate([p[:, sk[h]] for p in ps], axis=1), rsel,
                         preferred_element_type=F32)
            a[h] = jnp.where(diag_mask, ad, a[h])

        qe = (q * jnp.exp2(b)).astype(_ACT)
        b_last = b[C - 1:C, :]
        kd = (k * jnp.exp2(b_last - b)).astype(_ACT)
        keep = jnp.exp2(b_last)
        for h in heads:
            v = v_ref[rows, sv[h]]
            st = st_ref[h]
            o = lax.dot_general(qe[:, sk[h]], st.astype(_ACT), nt, preferred_element_type=F32)
            o = o + jnp.dot(a[h].astype(_ACT), v, preferred_element_type=F32)
            upd = lax.dot_general(v, kd[:, sk[h]], (((0,), (0,)), ((), ())), preferred_element_type=F32)
            st_ref[h] = st * keep[:, sk[h]] + upd

            ms = jnp.mean(o * o, axis=-1, keepdims=True)
            y = o * lax.rsqrt(ms + GLA_NORM_EPS) * gng
            o_ref[rows, sv[h]] = (y * _silu(g_ref[rows, sv[h]].astype(F32))).astype(_ACT)
        return carry

    lax.fori_loop(0, R // C, chunk, 0, unroll=8)


def _gla(gq, gk, gv, gg, la, gng, B, S, R):
    T = gq.shape[0]
    ns = S // R
    tril, rsel = _gla_constants()
    tril = jnp.asarray(tril, _ACT)
    rsel = jnp.asarray(rsel, _ACT)
    row = lambda b, i: (b * ns + i, 0)
    const = lambda b, i: (0, 0)
    rows = lambda w: pl.BlockSpec((R, w), row)
    whole = lambda a: pl.BlockSpec(a.shape, const)
    return pl.pallas_call(
        _gla_kernel,
        grid=(B, ns),
        in_specs=[rows(GLA_KW), rows(GLA_KW), rows(GLA_W), rows(GLA_W), rows(GLA_KW),
                  whole(tril), whole(rsel), whole(gng)],
        out_specs=rows(GLA_W),
        out_shape=jax.ShapeDtypeStruct((T, GLA_W), _ACT),
        scratch_shapes=[pltpu.VMEM((GLA_HEADS, GLA_DV, GLA_DK), F32)],
        compiler_params=pltpu.CompilerParams(
            dimension_semantics=("arbitrary", "arbitrary"), vmem_limit_bytes=_VMEM_LIMIT),
        name="gla",
    )(gq, gk, gv, gg, la, tril, rsel, gng)


def _diff_kernel(qT_ref, qTn_ref, k_ref, vT_ref, g_ref, lq1_ref, lk1_ref, lq2_ref, lk2_ref, gcol_ref, o_ref,
                 s0_ref, s1_ref, bm0_ref, bm1_ref, m_ref, acc_ref, *, lam_init):
    s_refs, bm_refs = (s0_ref, s1_ref), (bm0_ref, bm1_ref)
    tq = qT_ref.shape[1]
    tk = s0_ref.shape[1]
    assert tq == 2 * tk
    qi = pl.program_id(2)
    nfull = 2 * qi
    comp_row = lax.broadcasted_iota(jnp.int32, qT_ref.shape, 0) < DIFF_DQK

    def components(q):
        zero = jnp.zeros_like(q)
        return jnp.where(comp_row, q, zero), jnp.where(comp_row, zero, q)

    qcs = components(qT_ref[...])
    acc_ref[...] = jnp.zeros_like(acc_ref)
    m_ref[...] = jnp.full_like(m_ref, -jnp.inf)

    def produce(ki, slot, c, w, lw=tk, q=None):
        ln = slice(w * lw, (w + 1) * lw)
        kb = k_ref[pl.ds(pl.multiple_of(ki * tk, tk), tk), :]
        s = jnp.dot(kb, (qcs if q is None else q)[c][:, ln], preferred_element_type=F32)
        s_refs[slot][c, :, ln] = s
        bm_refs[slot][c, :, ln] = jnp.max(s, axis=0, keepdims=True)

    def consume(ki, slot, c, w, masked=False, lw=tk):
        ln = slice(w * lw, (w + 1) * lw)
        vb = vT_ref[:, pl.ds(pl.multiple_of(ki * tk, tk), tk)]
        s = s_refs[slot][c, :, ln]
        if masked:
            key = lax.broadcasted_iota(jnp.int32, s.shape, 0)
            qry = lax.broadcasted_iota(jnp.int32, s.shape, 1)
            s = jnp.where(key <= qry, s, -jnp.inf)
            bm = jnp.max(s, axis=0, keepdims=True)
        else:
            bm = bm_refs[slot][c, :, ln]
        m_old = m_ref[c, :, ln]
        m_new = jnp.maximum(m_old, bm)
        p = jnp.exp2(s - m_new).astype(_ACT)
        acc_ref[c, :, ln] = (jnp.exp2(m_old - m_new) * acc_ref[c, :, ln]
                             + jnp.dot(vb, p, preferred_element_type=F32))
        m_ref[c, :, ln] = m_new

    lws = min(_DIFF_LANE_WINDOW, tk)

    def step(k_next, slot_next, k_cur, slot_cur):
        for c in range(2):
            for w in range(tq // lws):
                produce(k_next, slot_next, c, w, lws)
                consume(k_cur, slot_cur, c, w, lw=lws)

    @pl.when(qi == 0)
    def _():
        for c in range(2):
            for w in range(2):
                produce(0, 0, c, w)

    def pair(j):
        step(2 * j + 1, 1, 2 * j, 0)
        step(2 * j + 2, 0, 2 * j + 1, 1)

    def pairs(t, carry):
        for u in range(_DIFF_PAIRS_PER_TRIP):
            pair(_DIFF_PAIRS_PER_TRIP * t + u)
        return carry

    assert _DIFF_PAIRS_PER_TRIP == 4
    full = qi // 4
    lax.fori_loop(0, full, pairs, 0)
    rem = qi - 4 * full

    @pl.when(rem >= 2)
    def _():
        pair(4 * full)
        pair(4 * full + 1)

    @pl.when(rem % 2 == 1)
    def _():
        pair(qi - 1)

    qn = components(qTn_ref[...])
    produce(nfull + 1, 1, 0, 1)
    consume(nfull, 0, 0, 0, masked=True)
    consume(nfull, 0, 0, 1)
    produce(nfull + 1, 1, 1, 1)
    consume(nfull, 0, 1, 0, masked=True)
    produce(0, 0, 0, 0, q=qn)
    consume(nfull, 0, 1, 1)
    produce(0, 0, 0, 1, q=qn)
    consume(nfull + 1, 1, 0, 1, masked=True)
    produce(0, 0, 1, 0, q=qn)
    consume(nfull + 1, 1, 1, 1, masked=True)
    produce(0, 0, 1, 1, q=qn)

    lam = (jnp.exp(jnp.sum(lq1_ref[...] * lk1_ref[...], axis=-1, keepdims=True))
           - jnp.exp(jnp.sum(lq2_ref[...] * lk2_ref[...], axis=-1, keepdims=True)) + lam_init)
    l1 = acc_ref[0, DIFF_DV:DIFF_DV + 1, :]
    l2 = acc_ref[1, DIFF_DV:DIFF_DV + 1, :]
    oT = acc_ref[0, 0:DIFF_DV, :] / l1 - lam * (acc_ref[1, 0:DIFF_DV, :] / l2)
    ms = jnp.mean(oT * oT, axis=0, keepdims=True)
    yT = oT * lax.rsqrt(ms + DIFF_NORM_EPS) * (gcol_ref[...] * (1.0 - lam_init))
    o_ref[...] = (yT.T * _silu(g_ref[...].astype(F32))).astype(_ACT)


def _diff(dqT, dk, dvT, dg, lq1, lk1, lq2, lk2, gcol, B, S, tq, lam_init):
    T = dk.shape[0]
    nq = S // tq
    small = lambda a: pl.BlockSpec(a.shape, lambda b, h, i: (0, 0))
    blk = pl.BlockSpec((tq, DIFF_DV), lambda b, h, i: (b * nq + i, h))
    return pl.pallas_call(
        functools.partial(_diff_kernel, lam_init=lam_init),
        grid=(B, DIFF_HEADS, nq),
        in_specs=[pl.BlockSpec((None, DIFF_DV, tq), lambda b, h, i: (b, h, i)),
                  pl.BlockSpec((None, DIFF_DV, tq), lambda b, h, i: (b, h, jnp.minimum(i + 1, nq - 1))),
                  pl.BlockSpec((S, DIFF_DV), lambda b, h, i: (b, h)),
                  pl.BlockSpec((None, DIFF_VT_ROWS, S), lambda b, h, i: (b, h, 0)),
                  blk, small(lq1), small(lk1), small(lq2), small(lk2), small(gcol)],
        out_specs=blk,
        out_shape=jax.ShapeDtypeStruct((T, DIFF_W), _ACT),
        scratch_shapes=[pltpu.VMEM((2, tq // 2, tq), F32), pltpu.VMEM((2, tq // 2, tq), F32),
                        pltpu.VMEM((2, 1, tq), F32), pltpu.VMEM((2, 1, tq), F32),
                        pltpu.VMEM((2, 1, tq), F32), pltpu.VMEM((2, DIFF_VT_ROWS, tq), F32)],
        compiler_params=pltpu.CompilerParams(
            dimension_semantics=("arbitrary", "arbitrary", "arbitrary"), vmem_limit_bytes=_VMEM_LIMIT),
        name="diffattn",
    )(dqT, dqT, dk, dvT, dg, lq1, lk1, lq2, lk2, gcol)


def _outproj_kernel(gla_ref, dif_ref, xat_ref, x_ref, w_ref, g_ref, b_ref, o_ref, *, alpha):
    y = alpha * x_ref[...]
    y = y + jnp.dot(gla_ref[...], w_ref[0:GLA_W, :], preferred_element_type=F32)
    y = y + jnp.dot(dif_ref[...], w_ref[GLA_W:GLA_W + DIFF_W, :], preferred_element_type=F32)
    y = y + jnp.dot(xat_ref[...], w_ref[GLA_W + DIFF_W:GLA_W + DIFF_W + MEM_W, :], preferred_element_type=F32)
    mu = jnp.mean(y, axis=-1, keepdims=True)
    yc = y - mu
    var = jnp.mean(yc * yc, axis=-1, keepdims=True)
    o_ref[...] = yc * lax.rsqrt(var + LN_EPS) * g_ref[...] + b_ref[...]


def _outproj(gla, dif, xat, x2d, w_out, ln_g, ln_b, tm, alpha):
    T, D = x2d.shape
    row = lambda i: (i, 0)
    const = lambda i: (0, 0)
    rows = lambda w: pl.BlockSpec((tm, w), row)
    return pl.pallas_call(
        functools.partial(_outproj_kernel, alpha=alpha),
        grid=(T // tm,),
        in_specs=[rows(GLA_W), rows(DIFF_W), rows(MEM_W), rows(D),
                  pl.BlockSpec(w_out.shape, const, pipeline_mode=pl.Buffered(1)),
                  pl.BlockSpec(ln_g.shape, const), pl.BlockSpec(ln_b.shape, const)],
        out_specs=rows(D),
        out_shape=jax.ShapeDtypeStruct((T, D), F32),
        compiler_params=pltpu.CompilerParams(
            dimension_semantics=("arbitrary",), vmem_limit_bytes=_VMEM_LIMIT),
        name="outproj",
    )(gla, dif, xat, x2d, w_out, ln_g, ln_b)


def _rotary_tables(positions):
    B, S = positions.shape
    half = ROT_DIM // 2
    inv_freq = jnp.power(jnp.float32(ROPE_THETA), -(jnp.arange(0, ROT_DIM, 2, dtype=F32) / ROT_DIM))
    lane = np.arange(LANES) % DIFF_DQK
    lane_freq = jnp.where(lane < ROT_DIM, inv_freq[lane % half], 0.0)
    ang = positions.astype(F32).reshape(B * S, 1) * lane_freq[None, :]
    return jnp.cos(ang), jnp.sin(ang)


def _tile(default, n):
    t = min(default, n)
    assert n % t == 0, (n, t)
    return t


def kernel(x, mem, positions, w_in, w_gk_up, b_gk_up, gla_norm_g, lambda_q1, lambda_k1, lambda_q2,
           lambda_k2, diff_norm_g, w_mem_kv, w_out, ln_g, ln_b):
    B, S, D = x.shape
    M = mem.shape[1]
    depth = w_in.shape[0]
    T = B * S
    assert S % GLA_CHUNK == 0 and M % LANES == 0
    tm_in, r_gla, tq, tm_out = _tile(_TM_IN, S), _tile(_R_GLA, S), _tile(_TQ, S), _tile(_TM_OUT, S)
    alpha = (2.0 * depth) ** 0.25

    cos_t, sin_t = _rotary_tables(positions)
    mem2d = mem.reshape(B * M, D)
    h = x.reshape(T, D)

    o_lr = 2 * GLA_KW + 2 * GLA_W
    o_d = o_lr + GLA_LOWRANK
    o_m = o_d + 4 * DIFF_W
    for l in range(depth):
        w_all = _pack_w_in(w_in, l)
        wgk = jnp.pad(w_gk_up[l].astype(F32), ((0, LANES - GLA_LOWRANK), (0, 0)))
        wgk_hi, wgk_lo = _split_hi_lo(wgk)
        wgk3 = jnp.concatenate([wgk_hi, wgk_lo, wgk_hi], axis=0)
        bgk = b_gk_up[l].astype(F32).reshape(1, GLA_KW)

        mkT, mv = _memkv(mem2d, w_mem_kv[l].astype(_ACT), B, M)
        gq, gk, gv, gg, la, dqT, dk, dvT, dg, xat = _inproj(
            h, w_all, wgk3, bgk, cos_t, sin_t, mkT, mv, B, S, M, tm_in)
        gla = _gla(gq, gk, gv, gg, la, gla_norm_g[l].astype(F32).reshape(1, GLA_DV), B, S, r_gla)
        row64 = lambda a: a[l].astype(F32).reshape(1, DIFF_DQK)
        dif = _diff(dqT, dk, dvT, dg, row64(lambda_q1), row64(lambda_k1), row64(lambda_q2),
                    row64(lambda_k2), diff_norm_g[l].astype(F32).reshape(DIFF_DV, 1), B, S, tq,
                    _lambda_init(l))
        h = _outproj(gla, dif, xat, h, w_out[l].astype(_ACT), ln_g[l].astype(F32).reshape(1, D),
                     ln_b[l].astype(F32).reshape(1, D), tm_out, alpha)
    return h.reshape(B, S, D)
```
